```python
import jax
import jax.numpy as jnp
from jax import lax
import numpy as np

D_MODEL = 2048
BATCH = 2
SEQ = 4096
DEPTH = 2
DEC_BATCH = 32
DEC_SEQ = 8
PAST_LEN = 8192
PAGE_SIZE = 128

N_HEADS = 16
HEAD_DIM = 128
N_KV_GROUPS = 4
GROUP_HEADS = N_HEADS // N_KV_GROUPS
CMP_BLOCK = 32
CMP_STRIDE = 16
CMP_HIDDEN = 256
SEL_BLOCK = 64
SEL_TOP = 16
WINDOW = 512
Q_BLOCK = 128
CONV_DIM = D_MODEL // 2
CONV_W = 31
N_EXPERTS = 32
TOP_K = 4
D_EXPERT = D_MODEL
SWIGLU_LIMIT = 7.0
SWIGLU_ALPHA = 1.702
ROPE_THETA = 10000.0
LN_EPS = 1e-5
MOE_BLOCK = 128
NEG_INF = -1e30
FORCE_SCORE = 1e9
DEEPNORM_ALPHA = (2 * DEPTH) ** 0.25
DEEPNORM_BETA = (8 * DEPTH) ** -0.25
Q_DIM = N_HEADS * HEAD_DIM
KV_DIM = N_KV_GROUPS * HEAD_DIM
GATE_DIM = 3 * N_HEADS
IN_SIZES = (Q_DIM,) + (KV_DIM,) * 6 + (GATE_DIM, 2 * CONV_DIM, 2 * D_MODEL)
P_IN = sum(IN_SIZES)

kernel_name = 'hybrid_nsa_conformer_moe_decoder_step'


def layer_norm(x, g, b):
    xf = x.astype(jnp.float32)
    mu = jnp.mean(xf, -1, keepdims=True)
    var = jnp.mean(jnp.square(xf - mu), -1, keepdims=True)
    return ((xf - mu) * lax.rsqrt(var + LN_EPS) * g + b).astype(x.dtype)


def rope(x, pos):
    half = HEAD_DIM // 2
    inv = ROPE_THETA ** (-jnp.arange(half, dtype=jnp.float32) / half)
    ang = pos.astype(jnp.float32)[:, None] * inv[None, :]
    cos = jnp.cos(ang)[:, None, :]
    sin = jnp.sin(ang)[:, None, :]
    xf = x.astype(jnp.float32)
    x1, x2 = xf[..., :half], xf[..., half:]
    return jnp.concatenate([x1 * cos - x2 * sin, x2 * cos + x1 * sin], -1).astype(x.dtype)


def masked_softmax(s, mask):
    s = jnp.where(mask, s.astype(jnp.float32), NEG_INF)
    m = jnp.max(s, -1, keepdims=True)
    p = jnp.where(mask, jnp.exp(s - m), 0.0)
    return p / jnp.maximum(jnp.sum(p, -1, keepdims=True), 1e-30)


def cmp_to_sel(n_cmp, n_sel):
    cs = jnp.arange(n_cmp)[:, None] * CMP_STRIDE
    ss = jnp.arange(n_sel)[None, :] * SEL_BLOCK
    return ((cs < ss + SEL_BLOCK) & (cs + CMP_BLOCK > ss)).astype(jnp.float32)


def compress(rows, pe, w1, w2, n_cmp):
    B = rows.shape[0]
    r = CMP_BLOCK // CMP_STRIDE
    chunks = rows[:, :(n_cmp + r - 1) * CMP_STRIDE].reshape(B, n_cmp + r - 1, CMP_STRIDE, N_KV_GROUPS, HEAD_DIM)
    w1r = w1.reshape(r, CMP_STRIDE, HEAD_DIM, CMP_HIDDEN)
    h = jnp.einsum('ld,ldh->h', pe, w1)
    for j in range(r):
        h = h + jnp.einsum('bnsgd,sdh->bngh', chunks[:, j:j + n_cmp], w1r[j])
    return jnp.einsum('bngh,he->bnge', jax.nn.gelu(h), w2)


def cmp_branch(k_rows, v_rows, cmp_pe, cmp_w1, cmp_w2):
    L = k_rows.shape[1]
    n_cmp = (L - CMP_BLOCK) // CMP_STRIDE + 1
    ends = jnp.arange(n_cmp, dtype=jnp.int32) * CMP_STRIDE + CMP_BLOCK - 1
    kc = rope(compress(k_rows, cmp_pe[0], cmp_w1[0], cmp_w2[0], n_cmp), ends)
    vc = compress(v_rows, cmp_pe[1], cmp_w1[1], cmp_w2[1], n_cmp)
    return kc, vc, ends


def sel_blocks(rows):
    B, L = rows.shape[:2]
    n_sel = -(-L // SEL_BLOCK)
    rows = jnp.pad(rows, ((0, 0), (0, n_sel * SEL_BLOCK - L), (0, 0), (0, 0)))
    return rows.reshape(B, n_sel, SEL_BLOCK, N_KV_GROUPS, HEAD_DIM).transpose(0, 3, 1, 2, 4)


def nsa_attend(q, t, kc, vc, ends, ks_blk, vs_blk, kw, vw, sw, gates):
    B, Tq = q.shape[:2]
    scale = HEAD_DIM ** -0.5
    s_c = jnp.einsum('bqgrd,bngd->bgrqn', q, kc) * scale
    p_c = masked_softmax(s_c, ends[None, :] <= t[:, None])
    o_c = jnp.einsum('bgrqn,bngd->bqgrd', p_c.astype(vc.dtype), vc)
    n_sel = ks_blk.shape[2]
    imp = jnp.einsum('bgrqn,nj->bgqj', p_c, cmp_to_sel(kc.shape[1], n_sel))
    blk = jnp.arange(n_sel)[None, :]
    cur = (t // SEL_BLOCK)[:, None]
    forced = (blk == 0) | (blk == cur) | (blk == cur - 1)
    future = blk * SEL_BLOCK > t[:, None]
    imp = jnp.where(future, NEG_INF, jnp.where(forced, FORCE_SCORE, imp))
    n_top = min(SEL_TOP, n_sel)
    _, idx = lax.top_k(imp, n_top)
    bi = jnp.arange(B)[:, None, None, None]
    gi = jnp.arange(N_KV_GROUPS)[None, :, None, None]
    ks = ks_blk[bi, gi, idx].reshape(B, N_KV_GROUPS, Tq, n_top * SEL_BLOCK, HEAD_DIM)
    vs = vs_blk[bi, gi, idx].reshape(B, N_KV_GROUPS, Tq, n_top * SEL_BLOCK, HEAD_DIM)
    pos_s = (idx[..., None] * SEL_BLOCK + jnp.arange(SEL_BLOCK)).reshape(B, N_KV_GROUPS, Tq, n_top * SEL_BLOCK)
    s_s = jnp.einsum('bqgrd,bgqnd->bgrqn', q, ks) * scale
    p_s = masked_softmax(s_s, (pos_s <= t[None, None, :, None])[:, :, None])
    o_s = jnp.einsum('bgrqn,bgqnd->bqgrd', p_s.astype(vs.dtype), vs)
    dt = t[:, None] - sw[None, :]
    mask_w = (dt >= 0) & (dt <= WINDOW) & (sw[None, :] >= 0)
    s_w = jnp.einsum('bqgrd,bkgd->bgrqk', q, kw) * scale
    p_w = masked_softmax(s_w, mask_w)
    o_w = jnp.einsum('bgrqk,bkgd->bqgrd', p_w.astype(vw.dtype), vw)
    o = gates[..., 0:1] * o_c + gates[..., 1:2] * o_s + gates[..., 2:3] * o_w
    return o.reshape(B, Tq, Q_DIM)


def nsa_prompt(q, gates, kc, vc, ends, ks_blk, vs_blk, k_win, v_win):
    B, T = q.shape[:2]
    nb = T // Q_BLOCK
    pad = ((0, 0), (WINDOW, 0), (0, 0), (0, 0))
    kw_pad = jnp.pad(k_win, pad)
    vw_pad = jnp.pad(v_win, pad)

    def one_block(args):
        qb, gb, b = args
        t0 = b * Q_BLOCK
        t = t0 + jnp.arange(Q_BLOCK, dtype=jnp.int32)
        kw = lax.dynamic_slice_in_dim(kw_pad, t0, Q_BLOCK + WINDOW, axis=1)
        vw = lax.dynamic_slice_in_dim(vw_pad, t0, Q_BLOCK + WINDOW, axis=1)
        sw = t0 - WINDOW + jnp.arange(Q_BLOCK + WINDOW, dtype=jnp.int32)
        return nsa_attend(qb, t, kc, vc, ends, ks_blk, vs_blk, kw, vw, sw, gb)

    qb = jnp.swapaxes(q.reshape(B, nb, Q_BLOCK, N_KV_GROUPS, GROUP_HEADS, HEAD_DIM), 0, 1)
    gb = jnp.swapaxes(gates.reshape(B, nb, Q_BLOCK, N_KV_GROUPS, GROUP_HEADS, 3), 0, 1)
    o = lax.map(one_block, (qb, gb, jnp.arange(nb, dtype=jnp.int32)))
    return jnp.swapaxes(o, 0, 1).reshape(B, T, Q_DIM)


def conv_module(u, buf, dw_w, dw_b, cn_g, cn_b, w_pb):
    h = u[..., :CONV_DIM] * jax.nn.sigmoid(u[..., CONV_DIM:])
    hx = jnp.concatenate([buf.astype(h.dtype), h], 1)
    y = lax.conv_general_dilated(hx, dw_w[:, None, :].astype(hx.dtype), (1,), 'VALID',
                                 dimension_numbers=('NWC', 'WIO', 'NWC'),
                                 feature_group_count=CONV_DIM) + dw_b
    y = jax.nn.silu(layer_norm(y, cn_g, cn_b))
    return y @ w_pb, hx[:, -(CONV_W - 1):]


def clamped_swiglu(h):
    x_glu = jnp.minimum(h[..., ::2], SWIGLU_LIMIT)
    x_lin = jnp.clip(h[..., 1::2], -SWIGLU_LIMIT, SWIGLU_LIMIT)
    return x_glu * jax.nn.sigmoid(SWIGLU_ALPHA * x_glu) * (x_lin + 1.0)


def moe(x, router_w, router_b, w_up, b_up, w_down, b_down):
    shp = x.shape
    xt = x.reshape(-1, D_MODEL)
    N = xt.shape[0]
    logits = (xt @ router_w + router_b).astype(jnp.float32)
    top_v, top_e = lax.top_k(logits, TOP_K)
    wts = jax.nn.softmax(top_v, -1)
    NK = N * TOP_K
    e_flat = top_e.reshape(-1)
    w_flat = wts.reshape(-1)
    tok = jnp.repeat(jnp.arange(N, dtype=jnp.int32), TOP_K)
    order = jnp.argsort(e_flat)
    e_sorted = e_flat[order]
    counts = jnp.zeros((N_EXPERTS,), jnp.int32).at[e_flat].add(1)
    padded = (counts + MOE_BLOCK - 1) // MOE_BLOCK * MOE_BLOCK
    start = jnp.cumsum(counts) - counts
    pend = jnp.cumsum(padded)
    pstart = pend - padded
    dest = pstart[e_sorted] + (jnp.arange(NK, dtype=jnp.int32) - start[e_sorted])
    NP = -(-(NK + N_EXPERTS * (MOE_BLOCK - 1)) // MOE_BLOCK) * MOE_BLOCK
    nb = NP // MOE_BLOCK
    tok_pad = jnp.zeros((NP,), jnp.int32).at[dest].set(tok[order])
    w_pad = jnp.zeros((NP,), jnp.float32).at[dest].set(w_flat[order])
    x_pad = xt[tok_pad].reshape(nb, MOE_BLOCK, D_MODEL)
    blk_e = jnp.minimum(jnp.searchsorted(pend, jnp.arange(nb, dtype=jnp.int32) * MOE_BLOCK, side='right'), N_EXPERTS - 1)

    def run(args):
        xb, e = args
        h = xb @ w_up[e] + b_up[e]
        return clamped_swiglu(h) @ w_down[e] + b_down[e]

    y_pad = lax.map(run, (x_pad, blk_e)).reshape(NP, D_MODEL)
    out = jax.ops.segment_sum(y_pad * w_pad[:, None].astype(y_pad.dtype), tok_pad, num_segments=N)
    return out.reshape(shp)


def mixer_inputs(x, pos, w_in, b_in):
    B, T, _ = x.shape
    z = x @ w_in + b_in
    parts = []
    off = 0
    for sz in IN_SIZES:
        parts.append(z[..., off:off + sz])
        off += sz
    q, kc, vc, ks, vs, kw, vw, g_nsa, u_conv, g_br = parts
    q = rope(q.reshape(B, T, N_HEADS, HEAD_DIM), pos).reshape(B, T, N_KV_GROUPS, GROUP_HEADS, HEAD_DIM)

    def kv(a):
        return a.reshape(B, T, N_KV_GROUPS, HEAD_DIM)

    gates = jax.nn.sigmoid(g_nsa).reshape(B, T, N_KV_GROUPS, GROUP_HEADS, 3)
    g_br = jax.nn.sigmoid(g_br)
    return (q, kv(kc), kv(vc), rope(kv(ks), pos), kv(vs), rope(kv(kw), pos), kv(vw),
            gates, u_conv, g_br[..., :D_MODEL], g_br[..., D_MODEL:])


def residual_update(x, o_attn, y_conv, g_a, g_b, w_oa, w_out, ln1_g, ln1_b,
                    router_w, router_b, w_up, b_up, w_down, b_down, ln2_g, ln2_b):
    mix = (g_a * (o_attn @ w_oa) + g_b * y_conv) @ w_out
    x = layer_norm(DEEPNORM_ALPHA * x + mix, ln1_g, ln1_b)
    return layer_norm(DEEPNORM_ALPHA * x + moe(x, router_w, router_b, w_up, b_up, w_down, b_down), ln2_g, ln2_b)


def gather_pages(cache, page_table):
    rows = cache[page_table]
    return rows.reshape(rows.shape[0], -1, N_KV_GROUPS, HEAD_DIM)


def setup_inputs(seed: int = 0) -> dict:
    key = jax.random.key(seed)
    keys = iter(list(jax.random.split(key, 48)))

    def nrm(shape, scale):
        return jax.random.normal(next(keys), shape, jnp.float32) * scale

    n_pages = PAST_LEN // PAGE_SIZE
    n_pool = (DEC_BATCH * n_pages * 5 + 3) // 4
    win_buf = min(WINDOW, PAST_LEN)
    cache_shape = (DEPTH, n_pool, PAGE_SIZE, N_KV_GROUPS, HEAD_DIM)
    page_table = jax.random.permutation(next(keys), n_pool)[:DEC_BATCH * n_pages].reshape(DEC_BATCH, n_pages).astype(jnp.int32)
    inputs = {
        'x_prompt': nrm((BATCH, SEQ, D_MODEL), 1.0),
        'x_sample': nrm((DEC_BATCH, DEC_SEQ, D_MODEL), 1.0),
        'cache_k_cmp': nrm(cache_shape, 1.0),
        'cache_v_cmp': nrm(cache_shape, 1.0),
        'cache_k_sel': nrm(cache_shape, 1.0),
        'cache_v_sel': nrm(cache_shape, 1.0),
        'state_win_k': nrm((DEPTH, DEC_BATCH, win_buf, N_KV_GROUPS, HEAD_DIM), 1.0),
        'state_win_v': nrm((DEPTH, DEC_BATCH, win_buf, N_KV_GROUPS, HEAD_DIM), 1.0),
        'state_conv': nrm((DEPTH, DEC_BATCH, CONV_W - 1, CONV_DIM), 0.5),
        'page_table': page_table,
        'w_in': nrm((DEPTH, D_MODEL, P_IN), D_MODEL ** -0.5),
        'b_in': nrm((DEPTH, P_IN), 0.01),
        'cmp_pe': nrm((DEPTH, 2, CMP_BLOCK, HEAD_DIM), 0.1),
        'cmp_w1': nrm((DEPTH, 2, CMP_BLOCK, HEAD_DIM, CMP_HIDDEN), (CMP_BLOCK * HEAD_DIM) ** -0.5),
        'cmp_w2': nrm((DEPTH, 2, CMP_HIDDEN, HEAD_DIM), CMP_HIDDEN ** -0.5),
        'w_oa': nrm((DEPTH, Q_DIM, D_MODEL), Q_DIM ** -0.5 * DEEPNORM_BETA),
        'dw_w': nrm((DEPTH, CONV_W, CONV_DIM), CONV_W ** -0.5),
        'dw_b': nrm((DEPTH, CONV_DIM), 0.01),
        'cn_g': 1.0 + nrm((DEPTH, CONV_DIM), 0.01),
        'cn_b': nrm((DEPTH, CONV_DIM), 0.01),
        'w_pb': nrm((DEPTH, CONV_DIM, D_MODEL), CONV_DIM ** -0.5 * DEEPNORM_BETA),
        'w_out': nrm((DEPTH, D_MODEL, D_MODEL), D_MODEL ** -0.5 * DEEPNORM_BETA),
        'ln1_g': 1.0 + nrm((DEPTH, D_MODEL), 0.01),
        'ln1_b': nrm((DEPTH, D_MODEL), 0.01),
        'router_w': nrm((DEPTH, D_MODEL, N_EXPERTS), D_MODEL ** -0.5),
        'router_b': nrm((DEPTH, N_EXPERTS), 0.01),
        'exp_w_up': nrm((DEPTH, N_EXPERTS, D_MODEL, 2 * D_EXPERT), D_MODEL ** -0.5),
        'exp_b_up': nrm((DEPTH, N_EXPERTS, 2 * D_EXPERT), 0.01),
        'exp_w_down': nrm((DEPTH, N_EXPERTS, D_EXPERT, D_MODEL), D_EXPERT ** -0.5 * DEEPNORM_BETA),
        'exp_b_down': nrm((DEPTH, N_EXPERTS, D_MODEL), 0.01),
        'ln2_g': 1.0 + nrm((DEPTH, D_MODEL), 0.01),
        'ln2_b': nrm((DEPTH, D_MODEL), 0.01),
    }
    return inputs


def reference(x_prompt, x_sample, cache_k_cmp, cache_v_cmp, cache_k_sel, cache_v_sel,
              state_win_k, state_win_v, state_conv, page_table,
              w_in, b_in, cmp_pe, cmp_w1, cmp_w2, w_oa, dw_w, dw_b, cn_g, cn_b, w_pb, w_out,
              ln1_g, ln1_b, router_w, router_b, exp_w_up, exp_b_up, exp_w_down, exp_b_down,
              ln2_g, ln2_b):
    B, T, _ = x_prompt.shape
    DB, S, _ = x_sample.shape
    past = page_table.shape[1] * cache_k_cmp.shape[2]
    wb = state_win_k.shape[2]
    pos_p = jnp.arange(T, dtype=jnp.int32)
    pos_s = past + jnp.arange(S, dtype=jnp.int32)
    xp, xs = x_prompt, x_sample
    kcp, vcp, ksp, vsp, wkp, wvp, cvp = [], [], [], [], [], [], []
    kcs, vcs, kss, vss, wks, wvs, cvs = [], [], [], [], [], [], []
    for l in range(DEPTH):
        q, k_cmp, v_cmp, k_sel, v_sel, k_win, v_win, gates, u, g_a, g_b = mixer_inputs(xp, pos_p, w_in[l], b_in[l])
        kc, vc, ends = cmp_branch(k_cmp, v_cmp, cmp_pe[l], cmp_w1[l], cmp_w2[l])
        o = nsa_prompt(q, gates, kc, vc, ends, sel_blocks(k_sel), sel_blocks(v_sel), k_win, v_win)
        y_conv, conv_buf = conv_module(u, jnp.zeros((B, CONV_W - 1, CONV_DIM), xp.dtype),
                                       dw_w[l], dw_b[l], cn_g[l], cn_b[l], w_pb[l])
        xp = residual_update(xp, o, y_conv, g_a, g_b, w_oa[l], w_out[l], ln1_g[l], ln1_b[l],
                             router_w[l], router_b[l], exp_w_up[l], exp_b_up[l], exp_w_down[l], exp_b_down[l],
                             ln2_g[l], ln2_b[l])
        nw = min(WINDOW, T)
        kcp.append(k_cmp)
        vcp.append(v_cmp)
        ksp.append(k_sel)
        vsp.append(v_sel)
        wkp.append(k_win[:, T - nw:])
        wvp.append(v_win[:, T - nw:])
        cvp.append(conv_buf)
        q, k_cmp, v_cmp, k_sel, v_sel, k_win, v_win, gates, u, g_a, g_b = mixer_inputs(xs, pos_s, w_in[l], b_in[l])
        k_cmp_all = jnp.concatenate([gather_pages(cache_k_cmp[l], page_table), k_cmp], 1)
        v_cmp_all = jnp.concatenate([gather_pages(cache_v_cmp[l], page_table), v_cmp], 1)
        k_sel_all = jnp.concatenate([gather_pages(cache_k_sel[l], page_table), k_sel], 1)
        v_sel_all = jnp.concatenate([gather_pages(cache_v_sel[l], page_table), v_sel], 1)
        kc, vc, ends = cmp_branch(k_cmp_all, v_cmp_all, cmp_pe[l], cmp_w1[l], cmp_w2[l])
        kw = jnp.concatenate([state_win_k[l], k_win], 1)
        vw = jnp.concatenate([state_win_v[l], v_win], 1)
        sw = past - wb + jnp.arange(wb + S, dtype=jnp.int32)
        o = nsa_attend(q, pos_s, kc, vc, ends, sel_blocks(k_sel_all), sel_blocks(v_sel_all), kw, vw, sw, gates)
        y_conv, conv_buf = conv_module(u, state_conv[l], dw_w[l], dw_b[l], cn_g[l], cn_b[l], w_pb[l])
        xs = residual_update(xs, o, y_conv, g_a, g_b, w_oa[l], w_out[l], ln1_g[l], ln1_b[l],
                             router_w[l], router_b[l], exp_w_up[l], exp_b_up[l], exp_w_down[l], exp_b_down[l],
                             ln2_g[l], ln2_b[l])
        kcs.append(k_cmp)
        vcs.append(v_cmp)
        kss.append(k_sel)
        vss.append(v_sel)
        wks.append(kw[:, S:])
        wvs.append(vw[:, S:])
        cvs.append(conv_buf)
    return (xp, xs,
            jnp.stack(kcp), jnp.stack(vcp), jnp.stack(ksp), jnp.stack(vsp),
            jnp.stack(wkp), jnp.stack(wvp), jnp.stack(cvp),
            jnp.stack(kcs), jnp.stack(vcs), jnp.stack(kss), jnp.stack(vss),
            jnp.stack(wks), jnp.stack(wvs), jnp.stack(cvs))
```

```python
import functools

import jax
import jax.numpy as jnp
from jax import lax
from jax.experimental import pallas as pl
from jax.experimental.pallas import tpu as pltpu

F32 = jnp.float32
BF16 = jnp.bfloat16

D_MODEL = 2048
N_HEADS = 16
HEAD_DIM = 128
N_GROUPS = 4
GROUP_HEADS = N_HEADS // N_GROUPS
GROUP_DIM = GROUP_HEADS * HEAD_DIM
KV_DIM = N_GROUPS * HEAD_DIM
CMP_BLOCK = 32
CMP_STRIDE = 16
CMP_HIDDEN = 256
SEL_BLOCK = 64
SEL_TOP = 16
WINDOW = 512
Q_BLOCK = 128
CONV_DIM = D_MODEL // 2
CONV_W = 31
CONV_PAD = 32
N_EXPERTS = 32
TOP_K = 4
D_EXPERT = D_MODEL
SWIGLU_LIMIT = 7.0
SWIGLU_ALPHA = 1.702
ROPE_THETA = 10000.0
LN_EPS = 1e-5
NEG_INF = -1e30
REMOVED = -3e38
FORCE_SCORE = 1e9
DEPTH = 2
DEEPNORM_ALPHA = (2 * DEPTH) ** 0.25
ATTN_SCALE = HEAD_DIM ** -0.5
GATE_DIM = 3 * N_HEADS

LANE = 128
TN_IN = 512
COL_Q = 0
COL_KS = 2048
COL_KW = 2560
COL_KC = 3072
COL_VC = 3584
COL_VS = 4096
COL_VW = 4608
COL_U = 5120
COL_GA = 7168
COL_GB = 9216
COL_GT = 11264
P_PAD = 11776
N_ROPE_TILES = COL_KC // TN_IN
N_PLAIN_TILES = (COL_GA - COL_KC) // TN_IN

MOE_BM = 512
MOE_TH = 512
SEL_TK = 512
PAGES_PER_STEP = 8
VMEM_LIMIT = 56 * 1024 * 1024


def _cparams(n_axes):
    return pltpu.CompilerParams(dimension_semantics=("arbitrary",) * n_axes,
                                vmem_limit_bytes=VMEM_LIMIT)


def _row_tile(n, cands):
    for c in cands:
        if n % c == 0:
            return c
    raise ValueError(f"no row tile for {n}")


def _sigmoid(x):
    return 1.0 / (1.0 + jnp.exp(-x))


def _layer_norm(x, g, b):
    mu = jnp.mean(x, -1, keepdims=True)
    xc = x - mu
    var = jnp.mean(xc * xc, -1, keepdims=True)
    return xc * lax.rsqrt(var + LN_EPS) * g + b


def _rope_apply(z, cos2, sin2):
    return z * cos2 + pltpu.roll(z, HEAD_DIM // 2, 1) * sin2


def _rope_tables(pos):
    half = HEAD_DIM // 2
    inv = ROPE_THETA ** (-jnp.arange(half, dtype=F32) / half)
    ang = pos.astype(F32)[:, None] * inv[None, :]
    c, s = jnp.cos(ang), jnp.sin(ang)
    return jnp.concatenate([c, c], -1), jnp.concatenate([-s, s], -1)


def _dot_nt(a, b):
    return lax.dot_general(a, b, (((1,), (1,)), ((), ())), preferred_element_type=F32)


def _inproj_kernel(x_ref, w_ref, b_ref, cos_ref, sin_ref, o_ref, xb_ref):
    j = pl.program_id(1)

    @pl.when(j == 0)
    def _():
        xb_ref[...] = x_ref[...].astype(BF16)

    z = jnp.dot(xb_ref[...], w_ref[...], preferred_element_type=F32) + b_ref[...]

    @pl.when(j < N_ROPE_TILES)
    def _():
        c = cos_ref[...]
        s = sin_ref[...]
        for h in range(TN_IN // HEAD_DIM):
            sl = slice(h * HEAD_DIM, (h + 1) * HEAD_DIM)
            o_ref[:, sl] = _rope_apply(z[:, sl], c, s)

    @pl.when((j >= N_ROPE_TILES) & (j < N_ROPE_TILES + N_PLAIN_TILES))
    def _():
        o_ref[...] = z

    @pl.when(j >= N_ROPE_TILES + N_PLAIN_TILES)
    def _():
        o_ref[...] = _sigmoid(z)


def _inproj(x, w, b, cos2, sin2):
    n = x.shape[0]
    tm = _row_tile(n, (768, 512, 256, 128, 64, 32, 16, 8))
    return pl.pallas_call(
        _inproj_kernel,
        grid=(n // tm, P_PAD // TN_IN),
        in_specs=[
            pl.BlockSpec((tm, D_MODEL), lambda i, j: (i, 0)),
            pl.BlockSpec((D_MODEL, TN_IN), lambda i, j: (0, j)),
            pl.BlockSpec((1, TN_IN), lambda i, j: (0, j)),
            pl.BlockSpec((tm, HEAD_DIM), lambda i, j: (i, 0)),
            pl.BlockSpec((tm, HEAD_DIM), lambda i, j: (i, 0)),
        ],
        out_specs=pl.BlockSpec((tm, TN_IN), lambda i, j: (i, j)),
        out_shape=jax.ShapeDtypeStruct((n, P_PAD), F32),
        scratch_shapes=[pltpu.VMEM((tm, D_MODEL), BF16)],
        compiler_params=_cparams(2),
        name="inproj",
    )(x, w, b, cos2, sin2)


def _prep_w_in(w_in, b_in):
    o_q, o_kc, o_vc, o_ks, o_vs, o_kw, o_vw = (0, 2048, 2560, 3072, 3584, 4096, 4608)
    o_gt = 5120
    o_u = o_gt + GATE_DIM
    o_gbr = o_u + 2 * CONV_DIM

    def build(a):
        def sl(o, n):
            return a[..., o:o + n]
        gates = sl(o_gt, GATE_DIM).reshape(a.shape[:-1] + (N_GROUPS, 3 * GROUP_HEADS))
        pad = [(0, 0)] * (gates.ndim - 1) + [(0, LANE - 3 * GROUP_HEADS)]
        gates = jnp.pad(gates, pad).reshape(a.shape[:-1] + (N_GROUPS * LANE,))
        return jnp.concatenate([
            sl(o_q, 2048), sl(o_ks, KV_DIM), sl(o_kw, KV_DIM),
            sl(o_kc, KV_DIM), sl(o_vc, KV_DIM), sl(o_vs, KV_DIM), sl(o_vw, KV_DIM), sl(o_u, 2 * CONV_DIM),
            sl(o_gbr, 2 * D_MODEL), gates], -1)

    return build(w_in).astype(BF16), build(b_in)[None, :]


CONV_RC = 32


def _conv_kernel(*refs, tt, has_prev):
    if has_prev:
        ua_ref, ub_ref, pa_ref, pb_ref, buf_ref, w_ref, b_ref, g_ref, be_ref, y_ref, nb_ref, hx_ref = refs
    else:
        ua_ref, ub_ref, buf_ref, w_ref, b_ref, g_ref, be_ref, y_ref, nb_ref, hx_ref = refs
    i = pl.program_id(1)
    hx_ref[CONV_PAD:CONV_PAD + tt, :] = ua_ref[...] * _sigmoid(ub_ref[...])
    if has_prev:
        @pl.when(i == 0)
        def _():
            hx_ref[0:CONV_PAD, :] = buf_ref[...]

        @pl.when(i > 0)
        def _():
            hx_ref[0:CONV_PAD, :] = pa_ref[...] * _sigmoid(pb_ref[...])
    else:
        hx_ref[0:CONV_PAD, :] = buf_ref[...]
    rc = min(CONV_RC, tt)
    off = CONV_PAD - (CONV_W - 1)
    for c in range(tt // rc):
        acc = jnp.zeros((rc, CONV_DIM), F32) + b_ref[...]
        for k in range(CONV_W):
            acc = acc + hx_ref[c * rc + off + k:c * rc + off + k + rc, :] * w_ref[k:k + 1, :]
        yn = _layer_norm(acc, g_ref[...], be_ref[...])
        y_ref[c * rc:(c + 1) * rc, :] = (yn * _sigmoid(yn)).astype(y_ref.dtype)
    nb_ref[...] = hx_ref[tt:tt + CONV_PAD, :]


def _conv(z, row0, nb_, t, buf, dw_w, dw_b, cn_g, cn_b):
    tt = min(t, 128)
    nt = t // tt
    has_prev = nt > 1
    rb0 = row0 // tt
    ca, cb = COL_U // CONV_DIM, COL_U // CONV_DIM + 1
    in_specs = [
        pl.BlockSpec((tt, CONV_DIM), lambda b, i: (rb0 + b * nt + i, ca)),
        pl.BlockSpec((tt, CONV_DIM), lambda b, i: (rb0 + b * nt + i, cb)),
    ]
    args = [z, z]
    if has_prev:
        pr = tt // CONV_PAD
        pb0 = row0 // CONV_PAD

        def prev_idx(b, i):
            return jnp.maximum(pb0 + (b * nt + i) * pr - 1, 0)
        in_specs += [
            pl.BlockSpec((CONV_PAD, CONV_DIM), lambda b, i: (prev_idx(b, i), ca)),
            pl.BlockSpec((CONV_PAD, CONV_DIM), lambda b, i: (prev_idx(b, i), cb)),
        ]
        args += [z, z]
    in_specs += [
        pl.BlockSpec((None, CONV_PAD, CONV_DIM), lambda b, i: (b, 0, 0)),
        pl.BlockSpec((CONV_PAD, CONV_DIM), lambda b, i: (0, 0)),
        pl.BlockSpec((1, CONV_DIM), lambda b, i: (0, 0)),
        pl.BlockSpec((1, CONV_DIM), lambda b, i: (0, 0)),
        pl.BlockSpec((1, CONV_DIM), lambda b, i: (0, 0)),
    ]
    w_pad = jnp.pad(dw_w, ((0, CONV_PAD - CONV_W), (0, 0)))
    args += [buf, w_pad, dw_b[None], cn_g[None], cn_b[None]]
    return pl.pallas_call(
        functools.partial(_conv_kernel, tt=tt, has_prev=has_prev),
        grid=(nb_, nt),
        in_specs=in_specs,
        out_specs=[
            pl.BlockSpec((tt, CONV_DIM), lambda b, i: (b * nt + i, 0)),
            pl.BlockSpec((None, CONV_PAD, CONV_DIM), lambda b, i: (b, 0, 0)),
        ],
        out_shape=[jax.ShapeDtypeStruct((nb_ * t, CONV_DIM), BF16 if tt % 16 == 0 else F32),
                   jax.ShapeDtypeStruct((nb_, CONV_PAD, CONV_DIM), F32)],
        scratch_shapes=[pltpu.VMEM((tt + CONV_PAD, CONV_DIM), F32)],
        compiler_params=_cparams(2),
        name="conv",
    )(*args)


CHUNK_K = CMP_STRIDE * HEAD_DIM


def _cmp1_prompt_kernel(*refs, nc):
    srcs = refs[:N_GROUPS]
    w_ref, out_ref, c_ref = refs[N_GROUPS:]
    for g in range(N_GROUPS):
        for s in range(CMP_STRIDE):
            c_ref[g * nc:(g + 1) * nc, s * HEAD_DIM:(s + 1) * HEAD_DIM] = (
                srcs[g][pl.ds(s, nc, stride=CMP_STRIDE), :])
    ab = jnp.dot(c_ref[...].astype(BF16), w_ref[...], preferred_element_type=F32)
    for g in range(N_GROUPS):
        out_ref[g] = ab[g * nc:(g + 1) * nc, :]


def _cmp1_prompt(z, b_, t, w01):
    nc = t // CMP_STRIDE
    assert COL_VC == COL_KC + KV_DIM
    return pl.pallas_call(
        functools.partial(_cmp1_prompt_kernel, nc=nc),
        grid=(2, b_),
        in_specs=[pl.BlockSpec((t, HEAD_DIM), functools.partial(
            lambda kv, b, g: (b, COL_KC // HEAD_DIM + kv * N_GROUPS + g), g=g)) for g in range(N_GROUPS)] + [
            pl.BlockSpec((None, CHUNK_K, 2 * CMP_HIDDEN), lambda kv, b: (kv, 0, 0)),
        ],
        out_specs=pl.BlockSpec((None, None, N_GROUPS, nc, 2 * CMP_HIDDEN), lambda kv, b: (kv, b, 0, 0, 0)),
        out_shape=jax.ShapeDtypeStruct((2, b_, N_GROUPS, nc, 2 * CMP_HIDDEN), F32),
        scratch_shapes=[pltpu.VMEM((N_GROUPS * nc, CHUNK_K), F32)],
        compiler_params=_cparams(2),
        name="cmp1_prompt",
    )(*([z] * N_GROUPS), w01)


PAGE_ROWS = 128
PAGE_FLAT = PAGE_ROWS * N_GROUPS
PAGE_CHUNKS = PAGE_ROWS // CMP_STRIDE


def _cmp1_sample_kernel(pt_ref, *refs, npg):
    k_pages = refs[:npg]
    v_pages = refs[npg:2 * npg]
    wk_ref, wv_ref, abk_ref, abv_ref, c_ref = refs[2 * npg:]
    rows = npg * PAGE_CHUNKS
    for pages, w_ref, out_ref in ((k_pages, wk_ref, abk_ref), (v_pages, wv_ref, abv_ref)):
        for p in range(npg):
            for g in range(N_GROUPS):
                r0 = g * rows + p * PAGE_CHUNKS
                for s in range(CMP_STRIDE):
                    c_ref[r0:r0 + PAGE_CHUNKS, s * HEAD_DIM:(s + 1) * HEAD_DIM] = (
                        pages[p][pl.ds(s * N_GROUPS + g, PAGE_CHUNKS, stride=CMP_STRIDE * N_GROUPS), :])
        ab = jnp.dot(c_ref[...].astype(BF16), w_ref[...], preferred_element_type=F32)
        for g in range(N_GROUPS):
            out_ref[g] = ab[g * rows:(g + 1) * rows, :]


def _page_spec(layer, p, npg):
    return pl.BlockSpec((None, None, PAGE_FLAT, HEAD_DIM),
                        lambda b, c, pt: (layer, pt[b, c * npg + p], 0, 0))


def _cmp1_sample(cache_k, cache_v, page_table, layer, wk, wv):
    db, n_pages = page_table.shape
    npg = PAGES_PER_STEP
    nc = n_pages * PAGE_CHUNKS
    rows = npg * PAGE_CHUNKS
    shp = jax.ShapeDtypeStruct((db, N_GROUPS, nc, 2 * CMP_HIDDEN), F32)
    ospec = pl.BlockSpec((None, N_GROUPS, rows, 2 * CMP_HIDDEN), lambda b, c, pt: (b, 0, c, 0))
    wspec = pl.BlockSpec((CHUNK_K, 2 * CMP_HIDDEN), lambda b, c, pt: (0, 0))
    return pl.pallas_call(
        functools.partial(_cmp1_sample_kernel, npg=npg),
        grid_spec=pltpu.PrefetchScalarGridSpec(
            num_scalar_prefetch=1,
            grid=(db, n_pages // npg),
            in_specs=[_page_spec(layer, p, npg) for p in range(npg)] * 2 + [wspec, wspec],
            out_specs=[ospec, ospec],
            scratch_shapes=[pltpu.VMEM((N_GROUPS * rows, CHUNK_K), F32)],
        ),
        out_shape=[shp, shp],
        compiler_params=_cparams(2),
        name="cmp1_sample",
    )(page_table, *([cache_k] * npg), *([cache_v] * npg), wk, wv)


def _cmp2_kernel(abk_ref, abv_ref, pek_ref, pev_ref, w1k_ref, w1v_ref, w2k_ref, w2v_ref,
                 cos_ref, sin_ref, kc_ref, vc_ref, *, nc):
    def one(ab_ref, pe_ref, w1_ref, w2_ref):
        ab = ab_ref[...]
        pe_term = jnp.dot(pe_ref[...], w1_ref[...], preferred_element_type=F32)[0:1, :]
        h = ab[:, :CMP_HIDDEN] + pltpu.roll(ab[:, CMP_HIDDEN:], nc - 1, 0) + pe_term
        return jnp.dot(jax.nn.gelu(h).astype(BF16), w2_ref[...], preferred_element_type=F32)

    kc_ref[...] = _rope_apply(one(abk_ref, pek_ref, w1k_ref, w2k_ref), cos_ref[...], sin_ref[...])
    vc_ref[...] = one(abv_ref, pev_ref, w1v_ref, w2v_ref)


def _cmp2(abk, abv, pek, pev, w1k, w1v, w2k, w2v, cos2, sin2):
    nb_, _, nc, _ = abk.shape
    abspec = pl.BlockSpec((None, None, nc, 2 * CMP_HIDDEN), lambda b, g: (b, g, 0, 0))
    ospec = pl.BlockSpec((None, None, nc, HEAD_DIM), lambda b, g: (b, g, 0, 0))
    shp = jax.ShapeDtypeStruct((nb_, N_GROUPS, nc, HEAD_DIM), F32)

    def full(a):
        return pl.BlockSpec(a.shape, lambda b, g: (0,) * a.ndim)
    return pl.pallas_call(
        functools.partial(_cmp2_kernel, nc=nc),
        grid=(nb_, N_GROUPS),
        in_specs=[abspec, abspec, full(pek), full(pev), full(w1k), full(w1v), full(w2k), full(w2v),
                  full(cos2), full(sin2)],
        out_specs=[ospec, ospec],
        out_shape=[shp, shp],
        compiler_params=_cparams(2),
        name="cmp2",
    )(abk, abv, pek, pev, w1k, w1v, w2k, w2v, cos2, sin2)


def _masked_softmax(s, maskf):
    m = jnp.max(s, -1, keepdims=True)
    p = jnp.exp(s - m) * maskf
    return p / jnp.maximum(jnp.sum(p, -1, keepdims=True), 1e-30)


def _bias(maskf):
    return (maskf - 1.0) * (-NEG_INF)


def _split_dot(p, mat_bf16):
    hi = p.astype(BF16)
    lo = (p - hi.astype(F32)).astype(BF16)
    return (jnp.dot(hi, mat_bf16, preferred_element_type=F32)
            + jnp.dot(lo, mat_bf16, preferred_element_type=F32))


def _cmp_to_sel(n_rows, n_cols):
    cs = lax.broadcasted_iota(jnp.int32, (n_rows, n_cols), 0) * CMP_STRIDE
    ss = lax.broadcasted_iota(jnp.int32, (n_rows, n_cols), 1) * SEL_BLOCK
    return jnp.where((cs < ss + SEL_BLOCK) & (cs + CMP_BLOCK > ss), 1.0, 0.0).astype(BF16)


def _select_blocks(imp, tq, n_sel):
    blk = lax.broadcasted_iota(jnp.int32, imp.shape, 1)
    cur = tq // SEL_BLOCK
    forced = (blk == 0) | (blk == cur) | (blk == cur - 1)
    future = blk * SEL_BLOCK > tq
    work = jnp.where(future, NEG_INF, jnp.where(forced, FORCE_SCORE, imp))
    work = jnp.where(blk < n_sel, work, REMOVED)
    sel = jnp.zeros(imp.shape, F32)
    for _ in range(SEL_TOP):
        mx = jnp.max(work, -1, keepdims=True)
        first = jnp.min(jnp.where(work == mx, blk, imp.shape[1]), -1, keepdims=True)
        pick = blk == first
        sel = jnp.where(pick, 1.0, sel)
        work = jnp.where(pick, REMOVED, work)
    return jnp.where(future, 0.0, sel)


def _attn_prompt_kernel(q_ref, ks_ref, vs_ref, kw_ref, vw_ref, kc_ref, vc_ref, gt_ref, o_ref, *, t_len, nc):
    i = pl.program_id(2)
    t0 = i * Q_BLOCK
    rows = GROUP_HEADS * Q_BLOCK
    qf = q_ref[...] * ATTN_SCALE
    q = jnp.concatenate([qf[:, r * HEAD_DIM:(r + 1) * HEAD_DIM] for r in range(GROUP_HEADS)], 0).astype(BF16)
    tq = t0 + lax.broadcasted_iota(jnp.int32, (Q_BLOCK, 1), 0)

    def per_head(a):
        return jnp.concatenate([a] * GROUP_HEADS, 0)

    ends = lax.broadcasted_iota(jnp.int32, (Q_BLOCK, nc), 1) * CMP_STRIDE + (CMP_BLOCK - 1)
    mc = per_head(jnp.where(ends <= tq, 1.0, 0.0))
    s_c = _dot_nt(q, kc_ref[...].astype(BF16)) + _bias(mc)
    p_c = _masked_softmax(s_c, mc)
    o_c = jnp.dot(p_c.astype(BF16), vc_ref[...].astype(BF16), preferred_element_type=F32)
    p_sum = p_c[0:Q_BLOCK]
    for r in range(1, GROUP_HEADS):
        p_sum = p_sum + p_c[r * Q_BLOCK:(r + 1) * Q_BLOCK]
    n_sel = t_len // SEL_BLOCK
    imp = _split_dot(p_sum, _cmp_to_sel(nc, LANE))
    sel = _select_blocks(imp, tq, n_sel).astype(BF16)

    def sel_tile(kt, carry):
        m_i, l_i, acc = carry
        k0 = pl.multiple_of(kt * SEL_TK, SEL_TK)
        kb = ks_ref[pl.ds(k0, SEL_TK), :].astype(BF16)
        vb = vs_ref[pl.ds(k0, SEL_TK), :].astype(BF16)
        kpos = k0 + lax.broadcasted_iota(jnp.int32, (LANE, SEL_TK), 1)
        expand = jnp.where(lax.broadcasted_iota(jnp.int32, (LANE, SEL_TK), 0) == kpos // SEL_BLOCK, 1.0, 0.0)
        picked = jnp.dot(sel, expand.astype(BF16), preferred_element_type=F32)
        kq = k0 + lax.broadcasted_iota(jnp.int32, (Q_BLOCK, SEL_TK), 1)
        mk = per_head(jnp.where((picked > 0.5) & (kq <= tq), 1.0, 0.0))
        s = _dot_nt(q, kb) + _bias(mk)
        m_new = jnp.maximum(m_i, jnp.max(s, -1, keepdims=True))
        alpha = jnp.exp(m_i - m_new)
        p = jnp.exp(s - m_new) * mk
        l_new = alpha * l_i + jnp.sum(p, -1, keepdims=True)
        acc_new = alpha * acc + jnp.dot(p.astype(BF16), vb, preferred_element_type=F32)
        return m_new, l_new, acc_new

    n_tiles = (t0 + Q_BLOCK + SEL_TK - 1) // SEL_TK
    init = (jnp.full((rows, 1), NEG_INF, F32), jnp.zeros((rows, 1), F32), jnp.zeros((rows, HEAD_DIM), F32))
    _, l_s, acc_s = lax.fori_loop(0, n_tiles, sel_tile, init)
    o_s = acc_s / jnp.maximum(l_s, 1e-30)

    wk = WINDOW + Q_BLOCK
    ws = pl.multiple_of(jnp.maximum(t0 - WINDOW, 0), Q_BLOCK)
    dt = tq - (ws + lax.broadcasted_iota(jnp.int32, (Q_BLOCK, wk), 1))
    mw = per_head(jnp.where((dt >= 0) & (dt <= WINDOW), 1.0, 0.0))
    s_w = _dot_nt(q, kw_ref[pl.ds(ws, wk), :].astype(BF16)) + _bias(mw)
    p_w = _masked_softmax(s_w, mw)
    o_w = jnp.dot(p_w.astype(BF16), vw_ref[pl.ds(ws, wk), :].astype(BF16), preferred_element_type=F32)

    gt = gt_ref[...]
    for r in range(GROUP_HEADS):
        sl = slice(r * Q_BLOCK, (r + 1) * Q_BLOCK)
        o = (gt[:, 3 * r:3 * r + 1] * o_c[sl] + gt[:, 3 * r + 1:3 * r + 2] * o_s[sl]
             + gt[:, 3 * r + 2:3 * r + 3] * o_w[sl])
        o_ref[:, r * HEAD_DIM:(r + 1) * HEAD_DIM] = o.astype(BF16)


def _attn_prompt(z, kc, vc, b_, t):
    nq = t // Q_BLOCK
    nc = kc.shape[2]

    def kv_spec(col):
        return pl.BlockSpec((t, HEAD_DIM), lambda b, g, i: (b, col // HEAD_DIM + g))
    cspec = pl.BlockSpec((None, None, nc, HEAD_DIM), lambda b, g, i: (b, g, 0, 0))
    return pl.pallas_call(
        functools.partial(_attn_prompt_kernel, t_len=t, nc=nc),
        grid=(b_, N_GROUPS, nq),
        in_specs=[
            pl.BlockSpec((Q_BLOCK, GROUP_DIM), lambda b, g, i: (b * nq + i, g)),
            kv_spec(COL_KS), kv_spec(COL_VS), kv_spec(COL_KW), kv_spec(COL_VW),
            cspec, cspec,
            pl.BlockSpec((Q_BLOCK, LANE), lambda b, g, i: (b * nq + i, COL_GT // LANE + g)),
        ],
        out_specs=pl.BlockSpec((Q_BLOCK, GROUP_DIM), lambda b, g, i: (b * nq + i, g)),
        out_shape=jax.ShapeDtypeStruct((b_ * t, N_HEADS * HEAD_DIM), BF16),
        compiler_params=_cparams(3),
        name="attn_prompt",
    )(z, z, z, z, z, kc, vc, z)


def _attn_sample_cmp_kernel(q_ref, kc_ref, vc_ref, oc_ref, sel_ref, *, s_len, nc, past, n_sel):
    hr = GROUP_HEADS * s_len
    qf = q_ref[...] * ATTN_SCALE
    tq = past + lax.broadcasted_iota(jnp.int32, (s_len, 1), 0)
    ends = lax.broadcasted_iota(jnp.int32, (s_len, nc), 1) * CMP_STRIDE + (CMP_BLOCK - 1)
    mc1 = jnp.where(ends <= tq, 1.0, 0.0)
    mc = jnp.concatenate([mc1] * GROUP_HEADS, 0)
    m2s = _cmp_to_sel(nc, sel_ref.shape[1])
    for g in range(N_GROUPS):
        q = jnp.concatenate([qf[:, (g * GROUP_HEADS + r) * HEAD_DIM:(g * GROUP_HEADS + r + 1) * HEAD_DIM]
                             for r in range(GROUP_HEADS)], 0).astype(BF16)
        s_c = _dot_nt(q, kc_ref[g].astype(BF16)) + _bias(mc)
        p_c = _masked_softmax(s_c, mc)
        oc_ref[g * hr:(g + 1) * hr, :] = jnp.dot(p_c.astype(BF16), vc_ref[g].astype(BF16),
                                                 preferred_element_type=F32)
        p_sum = p_c[0:s_len]
        for r in range(1, GROUP_HEADS):
            p_sum = p_sum + p_c[r * s_len:(r + 1) * s_len]
        sel = _select_blocks(_split_dot(p_sum, m2s), tq, n_sel)
        sel_ref[g * hr:(g + 1) * hr, :] = jnp.concatenate([sel] * GROUP_HEADS, 0).astype(BF16)


def _attn_sample_cmp(z, row0, kc, vc, db, s_len, past):
    nc = kc.shape[2]
    n_sel = -(-(past + s_len) // SEL_BLOCK)
    sel_lanes = -(-n_sel // LANE) * LANE
    rows = N_HEADS * s_len
    rb0 = row0 // s_len
    cspec = pl.BlockSpec((None, N_GROUPS, nc, HEAD_DIM), lambda b: (b, 0, 0, 0))
    return pl.pallas_call(
        functools.partial(_attn_sample_cmp_kernel, s_len=s_len, nc=nc, past=past, n_sel=n_sel),
        grid=(db,),
        in_specs=[pl.BlockSpec((s_len, N_HEADS * HEAD_DIM), lambda b: (rb0 + b, 0)), cspec, cspec],
        out_specs=[pl.BlockSpec((None, rows, HEAD_DIM), lambda b: (b, 0, 0)),
                   pl.BlockSpec((None, rows, sel_lanes), lambda b: (b, 0, 0))],
        out_shape=[jax.ShapeDtypeStruct((db, rows, HEAD_DIM), F32),
                   jax.ShapeDtypeStruct((db, rows, sel_lanes), BF16)],
        compiler_params=_cparams(1),
        name="attn_sample_cmp",
    )(z, kc, vc)


def _attn_sample_kernel(pt_ref, *refs, npg, s_len, past, wb):
    k_pages = refs[:npg]
    v_pages = refs[npg:2 * npg]
    (q_ref, kn_ref, vn_ref, kwn_ref, vwn_ref, swk_ref, swv_ref, oc_ref, sel_ref, gt_ref,
     o_ref, qs_ref, m_ref, l_ref, acc_ref) = refs[2 * npg:]
    c = pl.program_id(1)
    hr = GROUP_HEADS * s_len
    rows = N_GROUPS * hr
    row_id = lax.broadcasted_iota(jnp.int32, (rows, 1), 0)
    row_g = row_id // hr
    row_t = past + row_id % s_len

    @pl.when(c == 0)
    def _():
        qf = q_ref[...] * ATTN_SCALE
        for h in range(N_HEADS):
            qs_ref[h * s_len:(h + 1) * s_len, :] = qf[:, h * HEAD_DIM:(h + 1) * HEAD_DIM]
        m_ref[...] = jnp.full((rows, 1), NEG_INF, F32)
        l_ref[...] = jnp.zeros((rows, 1), F32)
        acc_ref[...] = jnp.zeros((rows, HEAD_DIM), F32)

    q = qs_ref[...].astype(BF16)
    sel = sel_ref[...]
    sel_lanes = sel.shape[1]
    flat = lax.broadcasted_iota(jnp.int32, (rows, PAGE_FLAT), 1)
    gmatch = flat % N_GROUPS == row_g

    def online(s, mk, v):
        m_i = m_ref[...]
        m_new = jnp.maximum(m_i, jnp.max(s, -1, keepdims=True))
        alpha = jnp.exp(m_i - m_new)
        p = jnp.exp(s - m_new) * mk
        l_ref[...] = alpha * l_ref[...] + jnp.sum(p, -1, keepdims=True)
        acc_ref[...] = alpha * acc_ref[...] + jnp.dot(p.astype(BF16), v, preferred_element_type=F32)
        m_ref[...] = m_new

    for p_ in range(npg):
        page = c * npg + p_
        blk = page * (PAGE_ROWS // SEL_BLOCK) + lax.broadcasted_iota(jnp.int32, (sel_lanes, PAGE_FLAT), 1) // (
            SEL_BLOCK * N_GROUPS)
        expand = jnp.where(lax.broadcasted_iota(jnp.int32, (sel_lanes, PAGE_FLAT), 0) == blk, 1.0, 0.0)
        picked = jnp.dot(sel, expand.astype(BF16), preferred_element_type=F32)
        mk = jnp.where((picked > 0.5) & gmatch, 1.0, 0.0)
        s = _dot_nt(q, k_pages[p_][...].astype(BF16)) + _bias(mk)
        online(s, mk, v_pages[p_][...].astype(BF16))

    @pl.when(c == pl.num_programs(1) - 1)
    def _():
        jn = lax.broadcasted_iota(jnp.int32, (hr, s_len), 1)
        tn = past + lax.broadcasted_iota(jnp.int32, (hr, s_len), 0) % s_len
        mn = jnp.where(past + jn <= tn, 1.0, 0.0)
        kn = kn_ref[...]
        vn = vn_ref[...]
        for g in range(N_GROUPS):
            sl = slice(g * hr, (g + 1) * hr)
            cl = slice(g * HEAD_DIM, (g + 1) * HEAD_DIM)
            s = _dot_nt(q[sl], kn[:, cl].astype(BF16)) + _bias(mn)
            m_i = m_ref[sl, :]
            m_new = jnp.maximum(m_i, jnp.max(s, -1, keepdims=True))
            alpha = jnp.exp(m_i - m_new)
            p = jnp.exp(s - m_new) * mn
            l_ref[sl, :] = alpha * l_ref[sl, :] + jnp.sum(p, -1, keepdims=True)
            acc_ref[sl, :] = alpha * acc_ref[sl, :] + jnp.dot(p.astype(BF16), vn[:, cl].astype(BF16),
                                                               preferred_element_type=F32)
        o_s = acc_ref[...] / jnp.maximum(l_ref[...], 1e-30)

        wflat = lax.broadcasted_iota(jnp.int32, (rows, wb * N_GROUPS), 1)
        dt = row_t - (past - wb + wflat // N_GROUPS)
        mw = jnp.where((dt >= 0) & (dt <= WINDOW) & (wflat % N_GROUPS == row_g), 1.0, 0.0)
        s_w = _dot_nt(q, swk_ref[...].astype(BF16)) + _bias(mw)
        m_w = jnp.max(s_w, -1, keepdims=True)
        kwn = kwn_ref[...]
        vwn = vwn_ref[...]
        o_w_parts = []
        for g in range(N_GROUPS):
            sl = slice(g * hr, (g + 1) * hr)
            cl = slice(g * HEAD_DIM, (g + 1) * HEAD_DIM)
            s_n = _dot_nt(q[sl], kwn[:, cl].astype(BF16)) + _bias(mn)
            m_g = jnp.maximum(m_w[sl], jnp.max(s_n, -1, keepdims=True))
            p_o = jnp.exp(s_w[sl] - m_g) * mw[sl]
            p_n = jnp.exp(s_n - m_g) * mn
            den = jnp.sum(p_o, -1, keepdims=True) + jnp.sum(p_n, -1, keepdims=True)
            num = (jnp.dot(p_o.astype(BF16), swv_ref[...].astype(BF16), preferred_element_type=F32)
                   + jnp.dot(p_n.astype(BF16), vwn[:, cl].astype(BF16), preferred_element_type=F32))
            o_w_parts.append(num / jnp.maximum(den, 1e-30))

        gt = gt_ref[...]
        oc = oc_ref[...]
        for g in range(N_GROUPS):
            for r in range(GROUP_HEADS):
                h = g * GROUP_HEADS + r
                sl = slice(h * s_len, (h + 1) * s_len)
                sw = slice(r * s_len, (r + 1) * s_len)
                c0 = g * LANE + 3 * r
                o = (gt[:, c0:c0 + 1] * oc[sl] + gt[:, c0 + 1:c0 + 2] * o_s[sl]
                     + gt[:, c0 + 2:c0 + 3] * o_w_parts[g][sw])
                o_ref[:, h * HEAD_DIM:(h + 1) * HEAD_DIM] = o


def _attn_sample(z, row0, cache_k, cache_v, page_table, layer, swk, swv, oc, sel, s_len, past):
    db, n_pages = page_table.shape
    npg = PAGES_PER_STEP
    wb = swk.shape[2] // N_GROUPS
    rows = N_HEADS * s_len
    rb0 = row0 // s_len
    qd = N_HEADS * HEAD_DIM

    def zspec(width, col):
        return pl.BlockSpec((s_len, width), lambda b, c, pt: (rb0 + b, col // width))

    def bspec(a):
        return pl.BlockSpec((None,) + a.shape[1:], lambda b, c, pt: (b,) + (0,) * (a.ndim - 1))
    in_specs = [_page_spec(layer, p, npg) for p in range(npg)] * 2 + [
        zspec(qd, COL_Q), zspec(KV_DIM, COL_KS), zspec(KV_DIM, COL_VS), zspec(KV_DIM, COL_KW), zspec(KV_DIM, COL_VW),
        pl.BlockSpec((None, None) + swk.shape[2:], lambda b, c, pt: (layer, b, 0, 0)),
        pl.BlockSpec((None, None) + swv.shape[2:], lambda b, c, pt: (layer, b, 0, 0)),
        bspec(oc), bspec(sel), zspec(N_GROUPS * LANE, COL_GT)]
    return pl.pallas_call(
        functools.partial(_attn_sample_kernel, npg=npg, s_len=s_len, past=past, wb=wb),
        grid_spec=pltpu.PrefetchScalarGridSpec(
            num_scalar_prefetch=1,
            grid=(db, n_pages // npg),
            in_specs=in_specs,
            out_specs=pl.BlockSpec((s_len, qd), lambda b, c, pt: (b, 0)),
            scratch_shapes=[pltpu.VMEM((rows, HEAD_DIM), F32), pltpu.VMEM((rows, 1), F32),
                            pltpu.VMEM((rows, 1), F32), pltpu.VMEM((rows, HEAD_DIM), F32)],
        ),
        out_shape=jax.ShapeDtypeStruct((db * s_len, qd), F32),
        compiler_params=_cparams(2),
        name="attn_sample",
    )(page_table, *([cache_k] * npg), *([cache_v] * npg), z, z, z, z, z, swk, swv, oc, sel, z)


def _merge_kernel(o_ref, y_ref, woa_ref, wpb_ref, ga_ref, gb_ref, m_ref):
    a = jnp.dot(o_ref[...], woa_ref[...], preferred_element_type=F32)
    c = jnp.dot(y_ref[...], wpb_ref[...], preferred_element_type=F32)
    m_ref[...] = (ga_ref[...] * a + gb_ref[...] * c).astype(BF16)


def _merge(o, y, w_oa, w_pb, z):
    n = o.shape[0]
    tm = _row_tile(n, (768, 512, 256, 128, 64, 32, 16))
    tn = TN_IN
    return pl.pallas_call(
        _merge_kernel,
        grid=(n // tm, D_MODEL // tn),
        in_specs=[
            pl.BlockSpec((tm, o.shape[1]), lambda i, j: (i, 0)),
            pl.BlockSpec((tm, CONV_DIM), lambda i, j: (i, 0)),
            pl.BlockSpec((w_oa.shape[0], tn), lambda i, j: (0, j)),
            pl.BlockSpec((CONV_DIM, tn), lambda i, j: (0, j)),
            pl.BlockSpec((tm, tn), lambda i, j: (i, COL_GA // tn + j)),
            pl.BlockSpec((tm, tn), lambda i, j: (i, COL_GB // tn + j)),
        ],
        out_specs=pl.BlockSpec((tm, tn), lambda i, j: (i, j)),
        out_shape=jax.ShapeDtypeStruct((n, D_MODEL), BF16),
        compiler_params=_cparams(2),
        name="merge",
    )(o, y, w_oa, w_pb, z, z)


def _post_mixer_kernel(m_ref, x_ref, wout_ref, g_ref, b_ref, rw_ref, rb_ref, x1_ref, x1b_ref, te_ref, tw_ref):
    mix = jnp.dot(m_ref[...], wout_ref[...], preferred_element_type=F32)
    x1 = _layer_norm(DEEPNORM_ALPHA * x_ref[...] + mix, g_ref[...], b_ref[...])
    x1_ref[...] = x1
    x1b_ref[...] = x1.astype(BF16)
    logits = jnp.dot(x1, rw_ref[...], preferred_element_type=F32, precision=lax.Precision.HIGHEST) + rb_ref[...]
    lane = lax.broadcasted_iota(jnp.int32, logits.shape, 1)
    work = jnp.where(lane < N_EXPERTS, logits, REMOVED)
    te = jnp.zeros(logits.shape, jnp.int32)
    tv = jnp.full(logits.shape, REMOVED, F32)
    for k in range(TOP_K):
        mx = jnp.max(work, -1, keepdims=True)
        first = jnp.min(jnp.where(work == mx, lane, LANE), -1, keepdims=True)
        te = jnp.where(lane == k, first, te)
        tv = jnp.where(lane == k, mx, tv)
        work = jnp.where(lane == first, REMOVED, work)
    e = jnp.where(lane < TOP_K, jnp.exp(tv - jnp.max(tv, -1, keepdims=True)), 0.0)
    te_ref[...] = te
    tw_ref[...] = e / jnp.sum(e, -1, keepdims=True)


def _post_mixer(m, x, w_out, g, b, rw, rb):
    n = m.shape[0]
    tm = _row_tile(n, (256, 128, 64, 32, 16))
    row = pl.BlockSpec((tm, D_MODEL), lambda i: (i, 0))
    vec = pl.BlockSpec((1, D_MODEL), lambda i: (0, 0))
    small = pl.BlockSpec((tm, LANE), lambda i: (i, 0))
    return pl.pallas_call(
        _post_mixer_kernel,
        grid=(n // tm,),
        in_specs=[row, row, pl.BlockSpec((D_MODEL, D_MODEL), lambda i: (0, 0)), vec, vec,
                  pl.BlockSpec((D_MODEL, LANE), lambda i: (0, 0)), pl.BlockSpec((1, LANE), lambda i: (0, 0))],
        out_specs=[row, row, small, small],
        out_shape=[jax.ShapeDtypeStruct((n, D_MODEL), F32), jax.ShapeDtypeStruct((n, D_MODEL), BF16),
                   jax.ShapeDtypeStruct((n, LANE), jnp.int32), jax.ShapeDtypeStruct((n, LANE), F32)],
        compiler_params=_cparams(1),
        name="post_mixer",
    )(m, x, w_out, g, b, rw, rb)


def _moe_kernel(be_ref, nu_ref, x_ref, wg_ref, wl_ref, bg_ref, bl_ref, wd_ref, bd_ref, o_ref):
    i = pl.program_id(0)
    t = pl.program_id(1)

    @pl.when(i < nu_ref[0])
    def _():
        x = x_ref[...]
        hg = jnp.dot(x, wg_ref[...], preferred_element_type=F32) + bg_ref[...]
        hl = jnp.dot(x, wl_ref[...], preferred_element_type=F32) + bl_ref[...]
        glu = jnp.minimum(hg, SWIGLU_LIMIT)
        lin = jnp.clip(hl, -SWIGLU_LIMIT, SWIGLU_LIMIT)
        act = glu * _sigmoid(SWIGLU_ALPHA * glu) * (lin + 1.0)
        part = jnp.dot(act.astype(BF16), wd_ref[...], preferred_element_type=F32)

        @pl.when(t == 0)
        def _():
            o_ref[...] = part + bd_ref[...]

        @pl.when(t > 0)
        def _():
            o_ref[...] += part


def _moe(x_pad, blk_e, n_used, wg, wl, bg, bl, wd, bd):
    npad = x_pad.shape[0]
    nb = npad // MOE_BM
    nh = D_EXPERT // MOE_TH

    def blk(i, nu):
        return jnp.minimum(i, nu[0] - 1)

    def tt(i, t, nu):
        return jnp.where(i < nu[0], t, nh - 1)
    return pl.pallas_call(
        _moe_kernel,
        grid_spec=pltpu.PrefetchScalarGridSpec(
            num_scalar_prefetch=2,
            grid=(nb, nh),
            in_specs=[
                pl.BlockSpec((MOE_BM, D_MODEL), lambda i, t, be, nu: (blk(i, nu), 0)),
                pl.BlockSpec((None, D_MODEL, MOE_TH), lambda i, t, be, nu: (be[blk(i, nu)], 0, tt(i, t, nu))),
                pl.BlockSpec((None, D_MODEL, MOE_TH), lambda i, t, be, nu: (be[blk(i, nu)], 0, tt(i, t, nu))),
                pl.BlockSpec((None, 1, MOE_TH), lambda i, t, be, nu: (be[blk(i, nu)], 0, tt(i, t, nu))),
                pl.BlockSpec((None, 1, MOE_TH), lambda i, t, be, nu: (be[blk(i, nu)], 0, tt(i, t, nu))),
                pl.BlockSpec((None, MOE_TH, D_MODEL), lambda i, t, be, nu: (be[blk(i, nu)], tt(i, t, nu), 0)),
                pl.BlockSpec((None, 1, D_MODEL), lambda i, t, be, nu: (be[blk(i, nu)], 0, 0)),
            ],
            out_specs=pl.BlockSpec((MOE_BM, D_MODEL), lambda i, t, be, nu: (blk(i, nu), 0)),
        ),
        out_shape=jax.ShapeDtypeStruct((npad, D_MODEL), F32),
        compiler_params=_cparams(2),
        name="moe",
    )(blk_e, n_used, x_pad, wg, wl, bg, bl, wd, bd)


def _route(top_e, n):
    nk = n * TOP_K
    e_flat = top_e.reshape(-1)
    onehot = (e_flat[:, None] == jnp.arange(N_EXPERTS, dtype=jnp.int32)[None, :]).astype(jnp.int32)
    rank = jnp.take_along_axis(jnp.cumsum(onehot, 0), e_flat[:, None], 1)[:, 0] - 1
    counts = jnp.sum(onehot, 0)
    padded = (counts + MOE_BM - 1) // MOE_BM * MOE_BM
    pend = jnp.cumsum(padded)
    pstart = pend - padded
    dest = pstart[e_flat] + rank
    npad = -(-(nk + N_EXPERTS * (MOE_BM - 1)) // MOE_BM) * MOE_BM
    nb = npad // MOE_BM
    tok = jnp.repeat(jnp.arange(n, dtype=jnp.int32), TOP_K)
    tok_pad = jnp.zeros((npad,), jnp.int32).at[dest].set(tok)
    blk_e = jnp.minimum(jnp.searchsorted(pend, jnp.arange(nb, dtype=jnp.int32) * MOE_BM, side='right'),
                        N_EXPERTS - 1).astype(jnp.int32)
    n_used = (pend[-1] // MOE_BM).astype(jnp.int32).reshape(1)
    return dest.reshape(n, TOP_K), tok_pad, blk_e, n_used


def _post_moe_kernel(x_ref, y_ref, w_ref, g_ref, b_ref, o_ref):
    w = w_ref[...]
    moe = w[:, 0:1] * y_ref[0]
    for k in range(1, TOP_K):
        moe = moe + w[:, k:k + 1] * y_ref[k]
    o_ref[...] = _layer_norm(DEEPNORM_ALPHA * x_ref[...] + moe, g_ref[...], b_ref[...])


def _post_moe(x1, yg, tw, g, b):
    n = x1.shape[0]
    tm = _row_tile(n, (256, 128, 64, 32, 16))
    row = pl.BlockSpec((tm, D_MODEL), lambda i: (i, 0))
    vec = pl.BlockSpec((1, D_MODEL), lambda i: (0, 0))
    return pl.pallas_call(
        _post_moe_kernel,
        grid=(n // tm,),
        in_specs=[row, pl.BlockSpec((TOP_K, tm, D_MODEL), lambda i: (0, i, 0)),
                  pl.BlockSpec((tm, LANE), lambda i: (i, 0)), vec, vec],
        out_specs=row,
        out_shape=jax.ShapeDtypeStruct((n, D_MODEL), F32),
        compiler_params=_cparams(1),
        name="post_moe",
    )(x1, yg, tw, g, b)


def kernel(x_prompt, x_sample, cache_k_cmp, cache_v_cmp, cache_k_sel, cache_v_sel, state_win_k, state_win_v, state_conv, page_table, w_in, b_in, cmp_pe, cmp_w1, cmp_w2, w_oa, dw_w, dw_b, cn_g, cn_b, w_pb, w_out, ln1_g, ln1_b, router_w, router_b, exp_w_up, exp_b_up, exp_w_down, exp_b_down, ln2_g, ln2_b):
    b_, t, _ = x_prompt.shape
    db, s_len, _ = x_sample.shape
    depth = w_in.shape[0]
    n_pages = page_table.shape[1]
    past = n_pages * cache_k_cmp.shape[2]
    wb = state_win_k.shape[2]
    n_p = b_ * t
    n_s = db * s_len
    n = n_p + n_s
    nw = min(WINDOW, t)

    x = jnp.concatenate([x_prompt.reshape(n_p, D_MODEL), x_sample.reshape(n_s, D_MODEL)], 0)
    pos = jnp.concatenate([jnp.tile(jnp.arange(t, dtype=jnp.int32), b_),
                           jnp.tile(past + jnp.arange(s_len, dtype=jnp.int32), db)])
    cos2, sin2 = _rope_tables(pos)
    nc_p = t // CMP_STRIDE
    nc_s = n_pages * PAGE_CHUNKS
    cmp_cos_p, cmp_sin_p = _rope_tables(jnp.arange(nc_p, dtype=jnp.int32) * CMP_STRIDE + CMP_BLOCK - 1)
    cmp_cos_s, cmp_sin_s = _rope_tables(jnp.arange(nc_s, dtype=jnp.int32) * CMP_STRIDE + CMP_BLOCK - 1)

    def pages(c):
        return c.reshape(c.shape[0], c.shape[1], PAGE_FLAT, HEAD_DIM)
    ck_cmp, cv_cmp, ck_sel, cv_sel = pages(cache_k_cmp), pages(cache_v_cmp), pages(cache_k_sel), pages(cache_v_sel)
    swk_all = state_win_k.reshape(depth, db, wb * N_GROUPS, HEAD_DIM)
    swv_all = state_win_v.reshape(depth, db, wb * N_GROUPS, HEAD_DIM)
    conv_zero = jnp.zeros((b_, CONV_PAD, CONV_DIM), F32)
    conv_state = jnp.pad(state_conv, ((0, 0), (0, 0), (CONV_PAD - (CONV_W - 1), 0), (0, 0)))

    outs = [[] for _ in range(14)]
    for l in range(depth):
        w_in_r, b_in_r = _prep_w_in(w_in[l], b_in[l])
        z = _inproj(x, w_in_r, b_in_r, cos2, sin2)

        w1 = cmp_w1[l].reshape(2, CMP_BLOCK * HEAD_DIM, CMP_HIDDEN).astype(BF16)
        w01 = jnp.concatenate([w1[:, :CHUNK_K], w1[:, CHUNK_K:]], -1)
        pe8 = jnp.broadcast_to(cmp_pe[l].reshape(2, 1, CMP_BLOCK * HEAD_DIM), (2, 8, CMP_BLOCK * HEAD_DIM)).astype(BF16)
        w2 = cmp_w2[l].astype(BF16)
        ab_p = _cmp1_prompt(z, b_, t, w01)
        kc_p, vc_p = _cmp2(ab_p[0], ab_p[1], pe8[0], pe8[1], w1[0], w1[1], w2[0], w2[1], cmp_cos_p, cmp_sin_p)
        abk_s, abv_s = _cmp1_sample(ck_cmp, cv_cmp, page_table, l, w01[0], w01[1])
        kc_s, vc_s = _cmp2(abk_s, abv_s, pe8[0], pe8[1], w1[0], w1[1], w2[0], w2[1], cmp_cos_s, cmp_sin_s)

        o_p = _attn_prompt(z, kc_p, vc_p, b_, t)
        oc_s, sel_s = _attn_sample_cmp(z, n_p, kc_s, vc_s, db, s_len, past)
        o_s = _attn_sample(z, n_p, ck_sel, cv_sel, page_table, l, swk_all, swv_all, oc_s, sel_s, s_len, past)
        o = jnp.concatenate([o_p, o_s.astype(BF16)], 0)

        y_p, cb_p = _conv(z, 0, b_, t, conv_zero, dw_w[l], dw_b[l], cn_g[l], cn_b[l])
        y_s, cb_s = _conv(z, n_p, db, s_len, conv_state[l], dw_w[l], dw_b[l], cn_g[l], cn_b[l])
        y = jnp.concatenate([y_p, y_s.astype(BF16)], 0)

        m = _merge(o, y, w_oa[l].astype(BF16), w_pb[l].astype(BF16), z)
        rw = jnp.pad(router_w[l], ((0, 0), (0, LANE - N_EXPERTS)))
        rb = jnp.pad(router_b[l], (0, LANE - N_EXPERTS))[None]
        x1, x1b, te, tw = _post_mixer(m, x, w_out[l].astype(BF16), ln1_g[l][None], ln1_b[l][None], rw, rb)

        dest, tok_pad, blk_e, n_used = _route(te[:, :TOP_K], n)
        wu = exp_w_up[l].reshape(N_EXPERTS, D_MODEL, D_EXPERT, 2)
        bu = exp_b_up[l].reshape(N_EXPERTS, 1, D_EXPERT, 2)
        y_pad = _moe(x1b[tok_pad], blk_e, n_used, wu[..., 0].astype(BF16), wu[..., 1].astype(BF16),
                     bu[..., 0], bu[..., 1], exp_w_down[l].astype(BF16), exp_b_down[l][:, None, :])
        x = _post_moe(x1, y_pad[dest.T], tw, ln2_g[l][None], ln2_b[l][None])

        def zp(col):
            return z[:n_p, col:col + KV_DIM].reshape(b_, t, N_GROUPS, HEAD_DIM)

        def zs(col):
            return z[n_p:, col:col + KV_DIM].reshape(db, s_len, N_GROUPS, HEAD_DIM)
        vals = [zp(COL_KC), zp(COL_VC), zp(COL_KS), zp(COL_VS), zp(COL_KW)[:, t - nw:], zp(COL_VW)[:, t - nw:],
                cb_p[:, CONV_PAD - (CONV_W - 1):],
                zs(COL_KC), zs(COL_VC), zs(COL_KS), zs(COL_VS),
                jnp.concatenate([state_win_k[l], zs(COL_KW)], 1)[:, s_len:],
                jnp.concatenate([state_win_v[l], zs(COL_VW)], 1)[:, s_len:],
                cb_s[:, CONV_PAD - (CONV_W - 1):]]
        for lst, v in zip(outs, vals):
            lst.append(v)

    return (x[:n_p].reshape(b_, t, D_MODEL), x[n_p:].reshape(db, s_len, D_MODEL),
            *[jnp.stack(v) for v in outs])
```

```python
import functools

import jax
import jax.numpy as jnp
from jax import lax
from jax.experimental import pallas as pl
from jax.experimental.pallas import tpu as pltpu

F32 = jnp.float32
BF16 = jnp.bfloat16

D_MODEL = 2048
N_HEADS = 16
HEAD_DIM = 128
N_GROUPS = 4
GROUP_HEADS = N_HEADS // N_GROUPS
GROUP_DIM = GROUP_HEADS * HEAD_DIM
KV_DIM = N_GROUPS * HEAD_DIM
CMP_BLOCK = 32
CMP_STRIDE = 16
CMP_HIDDEN = 256
SEL_BLOCK = 64
SEL_TOP = 16
WINDOW = 512
Q_BLOCK = 128
CONV_DIM = D_MODEL // 2
CONV_W = 31
CONV_PAD = 32
N_EXPERTS = 32
TOP_K = 4
D_EXPERT = D_MODEL
SWIGLU_LIMIT = 7.0
SWIGLU_ALPHA = 1.702
ROPE_THETA = 10000.0
LN_EPS = 1e-5
NEG_INF = -1e30
REMOVED = -3e38
FORCE_SCORE = 1e9
DEPTH = 2
DEEPNORM_ALPHA = (2 * DEPTH) ** 0.25
ATTN_SCALE = HEAD_DIM ** -0.5
GATE_DIM = 3 * N_HEADS

LANE = 128
TN_IN = 512
COL_Q = 0
COL_KS = 2048
COL_KW = 2560
COL_KC = 3072
COL_VC = 3584
COL_VS = 4096
COL_VW = 4608
COL_U = 5120
COL_GA = 7168
COL_GB = 9216
COL_GT = 11264
P_PAD = 11776
N_ROPE_TILES = COL_KC // TN_IN
N_PLAIN_TILES = (COL_GA - COL_KC) // TN_IN

MOE_BM = 512
MOE_TH = 512
SEL_TK = 512
PAGES_PER_STEP = 8
VMEM_LIMIT = 56 * 1024 * 1024


def _cparams(n_axes):
    return pltpu.CompilerParams(dimension_semantics=("arbitrary",) * n_axes,
                                vmem_limit_bytes=VMEM_LIMIT)


def _row_tile(n, cands):
    for c in cands:
        if n % c == 0:
            return c
    raise ValueError(f"no row tile for {n}")


def _sigmoid(x):
    return 1.0 / (1.0 + jnp.exp(-x))


def _layer_norm(x, g, b):
    mu = jnp.mean(x, -1, keepdims=True)
    xc = x - mu
    var = jnp.mean(xc * xc, -1, keepdims=True)
    return xc * lax.rsqrt(var + LN_EPS) * g + b


def _rope_apply(z, cos2, sin2):
    return z * cos2 + pltpu.roll(z, HEAD_DIM // 2, 1) * sin2


def _rope_tables(pos):
    half = HEAD_DIM // 2
    inv = ROPE_THETA ** (-jnp.arange(half, dtype=F32) / half)
    ang = pos.astype(F32)[:, None] * inv[None, :]
    c, s = jnp.cos(ang), jnp.sin(ang)
    return jnp.concatenate([c, c], -1), jnp.concatenate([-s, s], -1)


def _dot_nt(a, b):
    return lax.dot_general(a, b, (((1,), (1,)), ((), ())), preferred_element_type=F32)


def _inproj_kernel(x_ref, w_ref, b_ref, cos_ref, sin_ref, o_ref, xb_ref):
    j = pl.program_id(1)

    @pl.when(j == 0)
    def _():
        xb_ref[...] = x_ref[...].astype(BF16)

    z = jnp.dot(xb_ref[...], w_ref[...], preferred_element_type=F32) + b_ref[...]

    @pl.when(j < N_ROPE_TILES)
    def _():
        c = cos_ref[...]
        s = sin_ref[...]
        for h in range(TN_IN // HEAD_DIM):
            sl = slice(h * HEAD_DIM, (h + 1) * HEAD_DIM)
            o_ref[:, sl] = _rope_apply(z[:, sl], c, s)

    @pl.when((j >= N_ROPE_TILES) & (j < N_ROPE_TILES + N_PLAIN_TILES))
    def _():
        o_ref[...] = z

    @pl.when(j >= N_ROPE_TILES + N_PLAIN_TILES)
    def _():
        o_ref[...] = _sigmoid(z)


def _inproj(x, w, b, cos2, sin2):
    n = x.shape[0]
    tm = _row_tile(n, (768, 512, 256, 128, 64, 32, 16, 8))
    return pl.pallas_call(
        _inproj_kernel,
        grid=(n // tm, P_PAD // TN_IN),
        in_specs=[
            pl.BlockSpec((tm, D_MODEL), lambda i, j: (i, 0)),
            pl.BlockSpec((D_MODEL, TN_IN), lambda i, j: (0, j)),
            pl.BlockSpec((1, TN_IN), lambda i, j: (0, j)),
            pl.BlockSpec((tm, HEAD_DIM), lambda i, j: (i, 0)),
            pl.BlockSpec((tm, HEAD_DIM), lambda i, j: (i, 0)),
        ],
        out_specs=pl.BlockSpec((tm, TN_IN), lambda i, j: (i, j)),
        out_shape=jax.ShapeDtypeStruct((n, P_PAD), F32),
        scratch_shapes=[pltpu.VMEM((tm, D_MODEL), BF16)],
        compiler_params=_cparams(2),
        name="inproj",
    )(x, w, b, cos2, sin2)


def _prep_w_in(w_in, b_in):
    o_q, o_kc, o_vc, o_ks, o_vs, o_kw, o_vw = (0, 2048, 2560, 3072, 3584, 4096, 4608)
    o_gt = 5120
    o_u = o_gt + GATE_DIM
    o_gbr = o_u + 2 * CONV_DIM

    def build(a):
        def sl(o, n):
            return a[..., o:o + n]
        gates = sl(o_gt, GATE_DIM).reshape(a.shape[:-1] + (N_GROUPS, 3 * GROUP_HEADS))
        pad = [(0, 0)] * (gates.ndim - 1) + [(0, LANE - 3 * GROUP_HEADS)]
        gates = jnp.pad(gates, pad).reshape(a.shape[:-1] + (N_GROUPS * LANE,))
        return jnp.concatenate([
            sl(o_q, 2048), sl(o_ks, KV_DIM), sl(o_kw, KV_DIM),
            sl(o_kc, KV_DIM), sl(o_vc, KV_DIM), sl(o_vs, KV_DIM), sl(o_vw, KV_DIM), sl(o_u, 2 * CONV_DIM),
            sl(o_gbr, 2 * D_MODEL), gates], -1)

    return build(w_in).astype(BF16), build(b_in)[None, :]


CONV_RC = 32


def _conv_kernel(*refs, tt, has_prev):
    if has_prev:
        ua_ref, ub_ref, pa_ref, pb_ref, buf_ref, w_ref, b_ref, g_ref, be_ref, y_ref, nb_ref, hx_ref = refs
    else:
        ua_ref, ub_ref, buf_ref, w_ref, b_ref, g_ref, be_ref, y_ref, nb_ref, hx_ref = refs
    i = pl.program_id(1)
    hx_ref[CONV_PAD:CONV_PAD + tt, :] = ua_ref[...] * _sigmoid(ub_ref[...])
    if has_prev:
        @pl.when(i == 0)
        def _():
            hx_ref[0:CONV_PAD, :] = buf_ref[...]

        @pl.when(i > 0)
        def _():
            hx_ref[0:CONV_PAD, :] = pa_ref[...] * _sigmoid(pb_ref[...])
    else:
        hx_ref[0:CONV_PAD, :] = buf_ref[...]
    rc = min(CONV_RC, tt)
    off = CONV_PAD - (CONV_W - 1)
    for c in range(tt // rc):
        acc = jnp.zeros((rc, CONV_DIM), F32) + b_ref[...]
        for k in range(CONV_W):
            acc = acc + hx_ref[c * rc + off + k:c * rc + off + k + rc, :] * w_ref[k:k + 1, :]
        yn = _layer_norm(acc, g_ref[...], be_ref[...])
        y_ref[c * rc:(c + 1) * rc, :] = (yn * _sigmoid(yn)).astype(y_ref.dtype)
    nb_ref[...] = hx_ref[tt:tt + CONV_PAD, :]


def _conv(z, row0, nb_, t, buf, dw_w, dw_b, cn_g, cn_b):
    tt = min(t, 128)
    nt = t // tt
    has_prev = nt > 1
    rb0 = row0 // tt
    ca, cb = COL_U // CONV_DIM, COL_U // CONV_DIM + 1
    in_specs = [
        pl.BlockSpec((tt, CONV_DIM), lambda b, i: (rb0 + b * nt + i, ca)),
        pl.BlockSpec((tt, CONV_DIM), lambda b, i: (rb0 + b * nt + i, cb)),
    ]
    args = [z, z]
    if has_prev:
        pr = tt // CONV_PAD
        pb0 = row0 // CONV_PAD

        def prev_idx(b, i):
            return jnp.maximum(pb0 + (b * nt + i) * pr - 1, 0)
        in_specs += [
            pl.BlockSpec((CONV_PAD, CONV_DIM), lambda b, i: (prev_idx(b, i), ca)),
            pl.BlockSpec((CONV_PAD, CONV_DIM), lambda b, i: (prev_idx(b, i), cb)),
        ]
        args += [z, z]
    in_specs += [
        pl.BlockSpec((None, CONV_PAD, CONV_DIM), lambda b, i: (b, 0, 0)),
        pl.BlockSpec((CONV_PAD, CONV_DIM), lambda b, i: (0, 0)),
        pl.BlockSpec((1, CONV_DIM), lambda b, i: (0, 0)),
        pl.BlockSpec((1, CONV_DIM), lambda b, i: (0, 0)),
        pl.BlockSpec((1, CONV_DIM), lambda b, i: (0, 0)),
    ]
    w_pad = jnp.pad(dw_w, ((0, CONV_PAD - CONV_W), (0, 0)))
    args += [buf, w_pad, dw_b[None], cn_g[None], cn_b[None]]
    return pl.pallas_call(
        functools.partial(_conv_kernel, tt=tt, has_prev=has_prev),
        grid=(nb_, nt),
        in_specs=in_specs,
        out_specs=[
            pl.BlockSpec((tt, CONV_DIM), lambda b, i: (b * nt + i, 0)),
            pl.BlockSpec((None, CONV_PAD, CONV_DIM), lambda b, i: (b, 0, 0)),
        ],
        out_shape=[jax.ShapeDtypeStruct((nb_ * t, CONV_DIM), BF16 if tt % 16 == 0 else F32),
                   jax.ShapeDtypeStruct((nb_, CONV_PAD, CONV_DIM), F32)],
        scratch_shapes=[pltpu.VMEM((tt + CONV_PAD, CONV_DIM), F32)],
        compiler_params=_cparams(2),
        name="conv",
    )(*args)


CHUNK_K = CMP_STRIDE * HEAD_DIM


def _cmp1_prompt_kernel(*refs, nc):
    srcs = refs[:N_GROUPS]
    w_ref, out_ref, c_ref = refs[N_GROUPS:]
    for g in range(N_GROUPS):
        for s in range(CMP_STRIDE):
            c_ref[g * nc:(g + 1) * nc, s * HEAD_DIM:(s + 1) * HEAD_DIM] = (
                srcs[g][pl.ds(s, nc, stride=CMP_STRIDE), :])
    ab = jnp.dot(c_ref[...].astype(BF16), w_ref[...], preferred_element_type=F32)
    for g in range(N_GROUPS):
        out_ref[g] = ab[g * nc:(g + 1) * nc, :]


def _cmp1_prompt(z, b_, t, w01):
    nc = t // CMP_STRIDE
    assert COL_VC == COL_KC + KV_DIM
    return pl.pallas_call(
        functools.partial(_cmp1_prompt_kernel, nc=nc),
        grid=(2, b_),
        in_specs=[pl.BlockSpec((t, HEAD_DIM), functools.partial(
            lambda kv, b, g: (b, COL_KC // HEAD_DIM + kv * N_GROUPS + g), g=g)) for g in range(N_GROUPS)] + [
            pl.BlockSpec((None, CHUNK_K, 2 * CMP_HIDDEN), lambda kv, b: (kv, 0, 0)),
        ],
        out_specs=pl.BlockSpec((None, None, N_GROUPS, nc, 2 * CMP_HIDDEN), lambda kv, b: (kv, b, 0, 0, 0)),
        out_shape=jax.ShapeDtypeStruct((2, b_, N_GROUPS, nc, 2 * CMP_HIDDEN), F32),
        scratch_shapes=[pltpu.VMEM((N_GROUPS * nc, CHUNK_K), F32)],
        compiler_params=_cparams(2),
        name="cmp1_prompt",
    )(*([z] * N_GROUPS), w01)


PAGE_ROWS = 128
PAGE_FLAT = PAGE_ROWS * N_GROUPS
PAGE_CHUNKS = PAGE_ROWS // CMP_STRIDE


def _cmp1_sample_kernel(pt_ref, *refs, npg):
    k_pages = refs[:npg]
    v_pages = refs[npg:2 * npg]
    wk_ref, wv_ref, abk_ref, abv_ref, c_ref = refs[2 * npg:]
    low = lax.broadcasted_iota(jnp.int32, (2 * N_GROUPS, HEAD_DIM), 0) < N_GROUPS
    half = CMP_STRIDE // 2
    for pages, w_ref, out_ref in ((k_pages, wk_ref, abk_ref), (v_pages, wv_ref, abv_ref)):
        for p in range(npg):
            for n2 in range(PAGE_CHUNKS // 2):
                r0 = (p * PAGE_CHUNKS + 2 * n2) * N_GROUPS
                for sp in range(half):
                    ja = (2 * n2 * half + sp) * 8
                    jb = ((2 * n2 + 1) * half + sp) * 8
                    a = pages[p][ja:ja + 8, :]
                    b = pages[p][jb:jb + 8, :]
                    c_ref[r0:r0 + 8, (2 * sp) * HEAD_DIM:(2 * sp + 1) * HEAD_DIM] = (
                        jnp.where(low, a, pltpu.roll(b, N_GROUPS, 0)))
                    c_ref[r0:r0 + 8, (2 * sp + 1) * HEAD_DIM:(2 * sp + 2) * HEAD_DIM] = (
                        jnp.where(low, pltpu.roll(a, N_GROUPS, 0), b))
        out_ref[...] = jnp.dot(c_ref[...].astype(BF16), w_ref[...], preferred_element_type=F32)


def _page_spec(layer, p, npg):
    return pl.BlockSpec((None, None, PAGE_FLAT, HEAD_DIM),
                        lambda b, c, pt: (layer, pt[b, c * npg + p], 0, 0))


def _cmp1_sample(cache_k, cache_v, page_table, layer, wk, wv):
    assert 2 * N_GROUPS == 8 and CMP_STRIDE % 2 == 0
    db, n_pages = page_table.shape
    npg = PAGES_PER_STEP
    rows = npg * PAGE_CHUNKS * N_GROUPS
    shp = jax.ShapeDtypeStruct((db, n_pages * PAGE_CHUNKS * N_GROUPS, 2 * CMP_HIDDEN), F32)
    ospec = pl.BlockSpec((None, rows, 2 * CMP_HIDDEN), lambda b, c, pt: (b, c, 0))
    wspec = pl.BlockSpec((CHUNK_K, 2 * CMP_HIDDEN), lambda b, c, pt: (0, 0))
    return pl.pallas_call(
        functools.partial(_cmp1_sample_kernel, npg=npg),
        grid_spec=pltpu.PrefetchScalarGridSpec(
            num_scalar_prefetch=1,
            grid=(db, n_pages // npg),
            in_specs=[_page_spec(layer, p, npg) for p in range(npg)] * 2 + [wspec, wspec],
            out_specs=[ospec, ospec],
            scratch_shapes=[pltpu.VMEM((rows, CHUNK_K), F32)],
        ),
        out_shape=[shp, shp],
        compiler_params=_cparams(2),
        name="cmp1_sample",
    )(page_table, *([cache_k] * npg), *([cache_v] * npg), wk, wv)


def _cmp2_kernel(abk_ref, abv_ref, pek_ref, pev_ref, w1k_ref, w1v_ref, w2k_ref, w2v_ref,
                 cos_ref, sin_ref, kc_ref, vc_ref, *, shift):
    def one(ab_ref, pe_ref, w1_ref, w2_ref):
        ab = ab_ref[...]
        pe_term = jnp.dot(pe_ref[...], w1_ref[...], preferred_element_type=F32)[0:1, :]
        h = ab[:, :CMP_HIDDEN] + pltpu.roll(ab[:, CMP_HIDDEN:], ab.shape[0] - shift, 0) + pe_term
        return jnp.dot(jax.nn.gelu(h).astype(BF16), w2_ref[...], preferred_element_type=F32)

    kc_ref[...] = _rope_apply(one(abk_ref, pek_ref, w1k_ref, w2k_ref), cos_ref[...], sin_ref[...])
    vc_ref[...] = one(abv_ref, pev_ref, w1v_ref, w2v_ref)


def _cmp2(abk, abv, pek, pev, w1k, w1v, w2k, w2v, cos2, sin2, shift):
    nb_, rows, _ = abk.shape
    abspec = pl.BlockSpec((None, rows, 2 * CMP_HIDDEN), lambda b: (b, 0, 0))
    ospec = pl.BlockSpec((None, rows, HEAD_DIM), lambda b: (b, 0, 0))
    shp = jax.ShapeDtypeStruct((nb_, rows, HEAD_DIM), F32)

    def full(a):
        return pl.BlockSpec(a.shape, lambda b: (0,) * a.ndim)
    return pl.pallas_call(
        functools.partial(_cmp2_kernel, shift=shift),
        grid=(nb_,),
        in_specs=[abspec, abspec, full(pek), full(pev), full(w1k), full(w1v), full(w2k), full(w2v),
                  full(cos2), full(sin2)],
        out_specs=[ospec, ospec],
        out_shape=[shp, shp],
        compiler_params=_cparams(1),
        name="cmp2",
    )(abk, abv, pek, pev, w1k, w1v, w2k, w2v, cos2, sin2)


def _masked_softmax(s, maskf):
    m = jnp.max(s, -1, keepdims=True)
    p = jnp.exp(s - m) * maskf
    return p / jnp.maximum(jnp.sum(p, -1, keepdims=True), 1e-30)


def _bias(maskf):
    return (maskf - 1.0) * (-NEG_INF)


def _split_dot(p, mat_bf16):
    hi = p.astype(BF16)
    lo = (p - hi.astype(F32)).astype(BF16)
    return (jnp.dot(hi, mat_bf16, preferred_element_type=F32)
            + jnp.dot(lo, mat_bf16, preferred_element_type=F32))


def _cmp_to_sel(n_rows, n_cols):
    cs = lax.broadcasted_iota(jnp.int32, (n_rows, n_cols), 0) * CMP_STRIDE
    ss = lax.broadcasted_iota(jnp.int32, (n_rows, n_cols), 1) * SEL_BLOCK
    return jnp.where((cs < ss + SEL_BLOCK) & (cs + CMP_BLOCK > ss), 1.0, 0.0).astype(BF16)


def _cmp_to_sel_t(n_rows, n_cols):
    ss = lax.broadcasted_iota(jnp.int32, (n_rows, n_cols), 0) * SEL_BLOCK
    cs = lax.broadcasted_iota(jnp.int32, (n_rows, n_cols), 1) * CMP_STRIDE
    return jnp.where((cs < ss + SEL_BLOCK) & (cs + CMP_BLOCK > ss), 1.0, 0.0).astype(BF16)


def _select_blocks(imp, tq, n_sel):
    blk = lax.broadcasted_iota(jnp.int32, imp.shape, 1)
    cur = tq // SEL_BLOCK
    forced = (blk == 0) | (blk == cur) | (blk == cur - 1)
    future = blk * SEL_BLOCK > tq
    work = jnp.where(future, NEG_INF, jnp.where(forced, FORCE_SCORE, imp))
    work = jnp.where(blk < n_sel, work, REMOVED)
    sel = jnp.zeros(imp.shape, F32)
    for _ in range(SEL_TOP):
        mx = jnp.max(work, -1, keepdims=True)
        first = jnp.min(jnp.where(work == mx, blk, imp.shape[1]), -1, keepdims=True)
        pick = blk == first
        sel = jnp.where(pick, 1.0, sel)
        work = jnp.where(pick, REMOVED, work)
    return jnp.where(future, 0.0, sel)


def _select_blocks_t(imp_t, t0, n_top):
    n_sel = imp_t.shape[0]
    blk = lax.broadcasted_iota(jnp.int32, imp_t.shape, 0)
    tq = t0 + lax.broadcasted_iota(jnp.int32, imp_t.shape, 1)
    cur = tq // SEL_BLOCK
    forced = (blk == 0) | (blk == cur) | (blk == cur - 1)
    future = blk * SEL_BLOCK > tq
    work = jnp.where(future, NEG_INF, jnp.where(forced, FORCE_SCORE, imp_t))
    cnt = jnp.zeros(imp_t.shape, F32)
    for k in range(n_sel):
        wk = work[k:k + 1, :]
        ge = jnp.where(wk >= work, 1.0, 0.0)
        gt = jnp.where(wk > work, 1.0, 0.0)
        cnt = cnt + jnp.where(blk > k, ge, gt)
    return jnp.where(future, 0.0, jnp.where(cnt < n_top, 1.0, 0.0))


def _attn_prompt_kernel(q_ref, ks_ref, vs_ref, kw_ref, vw_ref, kc_ref, vc_ref, gt_ref, e_ref, o_ref, *, t_len, nc):
    i = pl.program_id(2)
    t0 = pl.multiple_of(i * Q_BLOCK, Q_BLOCK)
    qf = q_ref[...] * ATTN_SCALE
    q = jnp.concatenate([qf[:, r * HEAD_DIM:(r + 1) * HEAD_DIM] for r in range(GROUP_HEADS)], 0).astype(BF16)
    tq = t0 + lax.broadcasted_iota(jnp.int32, (Q_BLOCK, 1), 0)

    def per_head(a):
        return jnp.concatenate([a] * GROUP_HEADS, 0)

    ends = lax.broadcasted_iota(jnp.int32, (Q_BLOCK, nc), 1) * CMP_STRIDE + (CMP_BLOCK - 1)
    mc = per_head(jnp.where(ends <= tq, 1.0, 0.0))
    s_c = _dot_nt(q, kc_ref[...].astype(BF16)) + _bias(mc)
    p_c = _masked_softmax(s_c, mc)
    o_c = jnp.dot(p_c.astype(BF16), vc_ref[...].astype(BF16), preferred_element_type=F32)
    p_sum = p_c[0:Q_BLOCK]
    for r in range(1, GROUP_HEADS):
        p_sum = p_sum + p_c[r * Q_BLOCK:(r + 1) * Q_BLOCK]
    n_sel = t_len // SEL_BLOCK
    m2s_t = _cmp_to_sel_t(LANE, nc)
    p_hi = p_sum.astype(BF16)
    p_lo = (p_sum - p_hi.astype(F32)).astype(BF16)
    imp_t = _dot_nt(m2s_t, p_hi) + _dot_nt(m2s_t, p_lo)
    sel_t = _select_blocks_t(imp_t[:n_sel], t0, SEL_TOP)
    if n_sel < LANE:
        sel_t = jnp.concatenate([sel_t, jnp.zeros((LANE - n_sel, Q_BLOCK), F32)], 0)
    sel = jnp.transpose(sel_t)

    blk_lane = lax.broadcasted_iota(jnp.int32, (Q_BLOCK, LANE), 1)
    sel_bias = jnp.where(blk_lane < t0 // SEL_BLOCK, (sel - 1.0) * (-NEG_INF), NEG_INF)
    qa = jnp.concatenate([q, per_head(sel_bias.astype(BF16))], 1)
    kj = lax.broadcasted_iota(jnp.int32, (Q_BLOCK, Q_BLOCK), 1)
    qi = lax.broadcasted_iota(jnp.int32, (Q_BLOCK, Q_BLOCK), 0)
    causal = per_head(jnp.where(kj <= qi, 0.0, NEG_INF))
    s_d = _dot_nt(q, ks_ref[pl.ds(t0, Q_BLOCK), :].astype(BF16)) + causal
    m_d = jnp.max(s_d, -1, keepdims=True)
    p_d = jnp.exp(s_d - m_d)
    init = (m_d, jnp.sum(p_d, -1, keepdims=True),
            jnp.dot(p_d.astype(BF16), vs_ref[pl.ds(t0, Q_BLOCK), :].astype(BF16), preferred_element_type=F32))

    def sel_tile(kt, carry):
        m_i, l_i, acc = carry
        k0 = pl.multiple_of(kt * SEL_TK, SEL_TK)
        ka = jnp.concatenate([ks_ref[pl.ds(k0, SEL_TK), :].astype(BF16), e_ref[pl.ds(k0, SEL_TK), :]], 1)
        s = _dot_nt(qa, ka)
        m_new = jnp.maximum(m_i, jnp.max(s, -1, keepdims=True))
        alpha = jnp.exp(m_i - m_new)
        p = jnp.exp(s - m_new)
        l_new = alpha * l_i + jnp.sum(p, -1, keepdims=True)
        acc_new = alpha * acc + jnp.dot(p.astype(BF16), vs_ref[pl.ds(k0, SEL_TK), :].astype(BF16),
                                        preferred_element_type=F32)
        return m_new, l_new, acc_new

    _, l_s, acc_s = lax.fori_loop(0, (t0 + SEL_TK - 1) // SEL_TK, sel_tile, init)
    o_s = acc_s / l_s

    wk = WINDOW + Q_BLOCK
    ws = pl.multiple_of(jnp.maximum(t0 - WINDOW, 0), Q_BLOCK)
    dt = tq - (ws + lax.broadcasted_iota(jnp.int32, (Q_BLOCK, wk), 1))
    bw = per_head(jnp.where((dt >= 0) & (dt <= WINDOW), 0.0, NEG_INF))
    s_w = _dot_nt(q, kw_ref[pl.ds(ws, wk), :].astype(BF16)) + bw
    p_w = jnp.exp(s_w - jnp.max(s_w, -1, keepdims=True))
    o_w = (jnp.dot(p_w.astype(BF16), vw_ref[pl.ds(ws, wk), :].astype(BF16), preferred_element_type=F32)
           / jnp.sum(p_w, -1, keepdims=True))

    gt = gt_ref[...]
    for r in range(GROUP_HEADS):
        sl = slice(r * Q_BLOCK, (r + 1) * Q_BLOCK)
        o = (gt[:, 3 * r:3 * r + 1] * o_c[sl] + gt[:, 3 * r + 1:3 * r + 2] * o_s[sl]
             + gt[:, 3 * r + 2:3 * r + 3] * o_w[sl])
        o_ref[:, r * HEAD_DIM:(r + 1) * HEAD_DIM] = o.astype(BF16)


def _attn_prompt(z, kc, vc, b_, t):
    nq = t // Q_BLOCK
    nc = kc.shape[1]
    assert t % SEL_TK == 0 and t // SEL_BLOCK <= LANE and (t // SEL_BLOCK) % 8 == 0
    e_tab = (jnp.arange(t, dtype=jnp.int32)[:, None] // SEL_BLOCK
             == jnp.arange(LANE, dtype=jnp.int32)[None, :]).astype(BF16)

    def kv_spec(col):
        return pl.BlockSpec((t, HEAD_DIM), lambda b, g, i: (b, col // HEAD_DIM + g))
    cspec = pl.BlockSpec((None, nc, HEAD_DIM), lambda b, g, i: (b * N_GROUPS + g, 0, 0))
    return pl.pallas_call(
        functools.partial(_attn_prompt_kernel, t_len=t, nc=nc),
        grid=(b_, N_GROUPS, nq),
        in_specs=[
            pl.BlockSpec((Q_BLOCK, GROUP_DIM), lambda b, g, i: (b * nq + i, g)),
            kv_spec(COL_KS), kv_spec(COL_VS), kv_spec(COL_KW), kv_spec(COL_VW),
            cspec, cspec,
            pl.BlockSpec((Q_BLOCK, LANE), lambda b, g, i: (b * nq + i, COL_GT // LANE + g)),
            pl.BlockSpec((t, LANE), lambda b, g, i: (0, 0)),
        ],
        out_specs=pl.BlockSpec((Q_BLOCK, GROUP_DIM), lambda b, g, i: (b * nq + i, g)),
        out_shape=jax.ShapeDtypeStruct((b_ * t, N_HEADS * HEAD_DIM), BF16),
        compiler_params=_cparams(3),
        name="attn_prompt",
    )(z, z, z, z, z, kc, vc, z, e_tab)


def _attn_sample_cmp_kernel(q_ref, kc_ref, vc_ref, oc_ref, sel_ref, *, s_len, nc, past, n_sel):
    hr = GROUP_HEADS * s_len
    qf = q_ref[...] * ATTN_SCALE
    tq = past + lax.broadcasted_iota(jnp.int32, (s_len, 1), 0)
    ends = lax.broadcasted_iota(jnp.int32, (s_len, nc), 1) * CMP_STRIDE + (CMP_BLOCK - 1)
    mc1 = jnp.where(ends <= tq, 1.0, 0.0)
    mc = jnp.concatenate([mc1] * GROUP_HEADS, 0)
    m2s = _cmp_to_sel(nc, sel_ref.shape[1])
    for g in range(N_GROUPS):
        q = jnp.concatenate([qf[:, (g * GROUP_HEADS + r) * HEAD_DIM:(g * GROUP_HEADS + r + 1) * HEAD_DIM]
                             for r in range(GROUP_HEADS)], 0).astype(BF16)
        kc = kc_ref[pl.ds(g, nc, stride=N_GROUPS), :].astype(BF16)
        vc = vc_ref[pl.ds(g, nc, stride=N_GROUPS), :].astype(BF16)
        s_c = _dot_nt(q, kc) + _bias(mc)
        p_c = _masked_softmax(s_c, mc)
        oc_ref[g * hr:(g + 1) * hr, :] = jnp.dot(p_c.astype(BF16), vc, preferred_element_type=F32)
        p_sum = p_c[0:s_len]
        for r in range(1, GROUP_HEADS):
            p_sum = p_sum + p_c[r * s_len:(r + 1) * s_len]
        sel = _select_blocks(_split_dot(p_sum, m2s), tq, n_sel)
        sel_ref[g * hr:(g + 1) * hr, :] = jnp.concatenate([sel] * GROUP_HEADS, 0).astype(BF16)


def _attn_sample_cmp(z, row0, kc, vc, db, s_len, past):
    nc = kc.shape[1] // N_GROUPS
    n_sel = -(-(past + s_len) // SEL_BLOCK)
    sel_lanes = -(-n_sel // LANE) * LANE
    rows = N_HEADS * s_len
    rb0 = row0 // s_len
    cspec = pl.BlockSpec((None, nc * N_GROUPS, HEAD_DIM), lambda b: (b, 0, 0))
    return pl.pallas_call(
        functools.partial(_attn_sample_cmp_kernel, s_len=s_len, nc=nc, past=past, n_sel=n_sel),
        grid=(db,),
        in_specs=[pl.BlockSpec((s_len, N_HEADS * HEAD_DIM), lambda b: (rb0 + b, 0)), cspec, cspec],
        out_specs=[pl.BlockSpec((None, rows, HEAD_DIM), lambda b: (b, 0, 0)),
                   pl.BlockSpec((None, rows, sel_lanes), lambda b: (b, 0, 0))],
        out_shape=[jax.ShapeDtypeStruct((db, rows, HEAD_DIM), F32),
                   jax.ShapeDtypeStruct((db, rows, sel_lanes), BF16)],
        compiler_params=_cparams(1),
        name="attn_sample_cmp",
    )(z, kc, vc)


def _attn_sample_kernel(pt_ref, *refs, npg, s_len, past, wb):
    k_pages = refs[:npg]
    v_pages = refs[npg:2 * npg]
    (q_ref, kn_ref, vn_ref, kwn_ref, vwn_ref, swk_ref, swv_ref, oc_ref, sel_ref, gt_ref,
     o_ref, qs_ref, m_ref, l_ref, acc_ref) = refs[2 * npg:]
    c = pl.program_id(1)
    hr = GROUP_HEADS * s_len
    rows = N_GROUPS * hr
    row_id = lax.broadcasted_iota(jnp.int32, (rows, 1), 0)
    row_g = row_id // hr
    row_t = past + row_id % s_len

    @pl.when(c == 0)
    def _():
        qf = q_ref[...] * ATTN_SCALE
        for h in range(N_HEADS):
            qs_ref[h * s_len:(h + 1) * s_len, :] = qf[:, h * HEAD_DIM:(h + 1) * HEAD_DIM]
        m_ref[...] = jnp.full((rows, 1), NEG_INF, F32)
        l_ref[...] = jnp.zeros((rows, 1), F32)
        acc_ref[...] = jnp.zeros((rows, HEAD_DIM), F32)

    q = qs_ref[...].astype(BF16)
    sel = sel_ref[...]
    sel_lanes = sel.shape[1]
    flat = lax.broadcasted_iota(jnp.int32, (rows, PAGE_FLAT), 1)
    gmatch = flat % N_GROUPS == row_g
    first_half = flat < SEL_BLOCK * N_GROUPS
    bpp = PAGE_ROWS // SEL_BLOCK
    pick = jnp.where(lax.broadcasted_iota(jnp.int32, (sel_lanes, LANE), 0)
                     == c * (npg * bpp) + lax.broadcasted_iota(jnp.int32, (sel_lanes, LANE), 1), 1.0, 0.0)
    col_bias = (jnp.dot(sel, pick.astype(BF16), preferred_element_type=F32) - 1.0) * (-NEG_INF)

    scores = []
    for p_ in range(npg):
        bias = jnp.where(gmatch, jnp.where(first_half, col_bias[:, bpp * p_:bpp * p_ + 1],
                                           col_bias[:, bpp * p_ + 1:bpp * p_ + 2]), NEG_INF)
        scores.append(_dot_nt(q, k_pages[p_][...].astype(BF16)) + bias)
    s_max = scores[0]
    for s in scores[1:]:
        s_max = jnp.maximum(s_max, s)
    m_i = m_ref[...]
    m_new = jnp.maximum(m_i, jnp.max(s_max, -1, keepdims=True))
    alpha = jnp.exp(m_i - m_new)
    p_sum = jnp.zeros((rows, PAGE_FLAT), F32)
    acc = alpha * acc_ref[...]
    for p_ in range(npg):
        p = jnp.exp(scores[p_] - m_new)
        p_sum = p_sum + p
        acc = acc + jnp.dot(p.astype(BF16), v_pages[p_][...].astype(BF16), preferred_element_type=F32)
    l_ref[...] = alpha * l_ref[...] + jnp.sum(p_sum, -1, keepdims=True)
    acc_ref[...] = acc
    m_ref[...] = m_new

    @pl.when(c == pl.num_programs(1) - 1)
    def _():
        jn = lax.broadcasted_iota(jnp.int32, (hr, s_len), 1)
        tn = past + lax.broadcasted_iota(jnp.int32, (hr, s_len), 0) % s_len
        bn = jnp.where(past + jn <= tn, 0.0, NEG_INF)
        kn = kn_ref[...]
        vn = vn_ref[...]
        for g in range(N_GROUPS):
            sl = slice(g * hr, (g + 1) * hr)
            cl = slice(g * HEAD_DIM, (g + 1) * HEAD_DIM)
            s = _dot_nt(q[sl], kn[:, cl].astype(BF16)) + bn
            m_i = m_ref[sl, :]
            m_new = jnp.maximum(m_i, jnp.max(s, -1, keepdims=True))
            alpha = jnp.exp(m_i - m_new)
            p = jnp.exp(s - m_new)
            l_ref[sl, :] = alpha * l_ref[sl, :] + jnp.sum(p, -1, keepdims=True)
            acc_ref[sl, :] = alpha * acc_ref[sl, :] + jnp.dot(p.astype(BF16), vn[:, cl].astype(BF16),
                                                               preferred_element_type=F32)
        o_s = acc_ref[...] / l_ref[...]

        wflat = lax.broadcasted_iota(jnp.int32, (rows, wb * N_GROUPS), 1)
        dt = row_t - (past - wb + wflat // N_GROUPS)
        bw = jnp.where((dt >= 0) & (dt <= WINDOW) & (wflat % N_GROUPS == row_g), 0.0, NEG_INF)
        s_w = _dot_nt(q, swk_ref[...].astype(BF16)) + bw
        m_w = jnp.max(s_w, -1, keepdims=True)
        kwn = kwn_ref[...]
        vwn = vwn_ref[...]
        o_w_parts = []
        for g in range(N_GROUPS):
            sl = slice(g * hr, (g + 1) * hr)
            cl = slice(g * HEAD_DIM, (g + 1) * HEAD_DIM)
            s_n = _dot_nt(q[sl], kwn[:, cl].astype(BF16)) + bn
            m_g = jnp.maximum(m_w[sl], jnp.max(s_n, -1, keepdims=True))
            p_o = jnp.exp(s_w[sl] - m_g)
            p_n = jnp.exp(s_n - m_g)
            den = jnp.sum(p_o, -1, keepdims=True) + jnp.sum(p_n, -1, keepdims=True)
            num = (jnp.dot(p_o.astype(BF16), swv_ref[...].astype(BF16), preferred_element_type=F32)
                   + jnp.dot(p_n.astype(BF16), vwn[:, cl].astype(BF16), preferred_element_type=F32))
            o_w_parts.append(num / den)

        gt = gt_ref[...]
        oc = oc_ref[...]
        for g in range(N_GROUPS):
            for r in range(GROUP_HEADS):
                h = g * GROUP_HEADS + r
                sl = slice(h * s_len, (h + 1) * s_len)
                sw = slice(r * s_len, (r + 1) * s_len)
                c0 = g * LANE + 3 * r
                o = (gt[:, c0:c0 + 1] * oc[sl] + gt[:, c0 + 1:c0 + 2] * o_s[sl]
                     + gt[:, c0 + 2:c0 + 3] * o_w_parts[g][sw])
                o_ref[:, h * HEAD_DIM:(h + 1) * HEAD_DIM] = o


def _attn_sample(z, row0, cache_k, cache_v, page_table, layer, swk, swv, oc, sel, s_len, past):
    db, n_pages = page_table.shape
    npg = PAGES_PER_STEP
    wb = swk.shape[2] // N_GROUPS
    rows = N_HEADS * s_len
    rb0 = row0 // s_len
    qd = N_HEADS * HEAD_DIM

    def zspec(width, col):
        return pl.BlockSpec((s_len, width), lambda b, c, pt: (rb0 + b, col // width))

    def bspec(a):
        return pl.BlockSpec((None,) + a.shape[1:], lambda b, c, pt: (b,) + (0,) * (a.ndim - 1))
    in_specs = [_page_spec(layer, p, npg) for p in range(npg)] * 2 + [
        zspec(qd, COL_Q), zspec(KV_DIM, COL_KS), zspec(KV_DIM, COL_VS), zspec(KV_DIM, COL_KW), zspec(KV_DIM, COL_VW),
        pl.BlockSpec((None, None) + swk.shape[2:], lambda b, c, pt: (layer, b, 0, 0)),
        pl.BlockSpec((None, None) + swv.shape[2:], lambda b, c, pt: (layer, b, 0, 0)),
        bspec(oc), bspec(sel), zspec(N_GROUPS * LANE, COL_GT)]
    return pl.pallas_call(
        functools.partial(_attn_sample_kernel, npg=npg, s_len=s_len, past=past, wb=wb),
        grid_spec=pltpu.PrefetchScalarGridSpec(
            num_scalar_prefetch=1,
            grid=(db, n_pages // npg),
            in_specs=in_specs,
            out_specs=pl.BlockSpec((s_len, qd), lambda b, c, pt: (b, 0)),
            scratch_shapes=[pltpu.VMEM((rows, HEAD_DIM), F32), pltpu.VMEM((rows, 1), F32),
                            pltpu.VMEM((rows, 1), F32), pltpu.VMEM((rows, HEAD_DIM), F32)],
        ),
        out_shape=jax.ShapeDtypeStruct((db * s_len, qd), F32),
        compiler_params=_cparams(2),
        name="attn_sample",
    )(page_table, *([cache_k] * npg), *([cache_v] * npg), z, z, z, z, z, swk, swv, oc, sel, z)


def _merge_kernel(o_ref, y_ref, woa_ref, wpb_ref, ga_ref, gb_ref, m_ref):
    a = jnp.dot(o_ref[...], woa_ref[...], preferred_element_type=F32)
    c = jnp.dot(y_ref[...], wpb_ref[...], preferred_element_type=F32)
    m_ref[...] = (ga_ref[...] * a + gb_ref[...] * c).astype(BF16)


def _merge(o, y, w_oa, w_pb, z):
    n = o.shape[0]
    tm = _row_tile(n, (768, 512, 256, 128, 64, 32, 16))
    tn = TN_IN
    return pl.pallas_call(
        _merge_kernel,
        grid=(n // tm, D_MODEL // tn),
        in_specs=[
            pl.BlockSpec((tm, o.shape[1]), lambda i, j: (i, 0)),
            pl.BlockSpec((tm, CONV_DIM), lambda i, j: (i, 0)),
            pl.BlockSpec((w_oa.shape[0], tn), lambda i, j: (0, j)),
            pl.BlockSpec((CONV_DIM, tn), lambda i, j: (0, j)),
            pl.BlockSpec((tm, tn), lambda i, j: (i, COL_GA // tn + j)),
            pl.BlockSpec((tm, tn), lambda i, j: (i, COL_GB // tn + j)),
        ],
        out_specs=pl.BlockSpec((tm, tn), lambda i, j: (i, j)),
        out_shape=jax.ShapeDtypeStruct((n, D_MODEL), BF16),
        compiler_params=_cparams(2),
        name="merge",
    )(o, y, w_oa, w_pb, z, z)


def _post_mixer_kernel(m_ref, x_ref, wout_ref, g_ref, b_ref, rw_ref, rb_ref, x1_ref, te_ref, tw_ref):
    mix = jnp.dot(m_ref[...], wout_ref[...], preferred_element_type=F32)
    x1 = _layer_norm(DEEPNORM_ALPHA * x_ref[...] + mix, g_ref[...], b_ref[...])
    x1_ref[...] = x1
    logits = jnp.dot(x1, rw_ref[...], preferred_element_type=F32, precision=lax.Precision.HIGHEST) + rb_ref[...]
    lane = lax.broadcasted_iota(jnp.int32, logits.shape, 1)
    work = jnp.where(lane < N_EXPERTS, logits, REMOVED)
    te = jnp.zeros(logits.shape, jnp.int32)
    tv = jnp.full(logits.shape, REMOVED, F32)
    for k in range(TOP_K):
        mx = jnp.max(work, -1, keepdims=True)
        first = jnp.min(jnp.where(work == mx, lane, LANE), -1, keepdims=True)
        te = jnp.where(lane == k, first, te)
        tv = jnp.where(lane == k, mx, tv)
        work = jnp.where(lane == first, REMOVED, work)
    e = jnp.where(lane < TOP_K, jnp.exp(tv - jnp.max(tv, -1, keepdims=True)), 0.0)
    te_ref[...] = te
    tw_ref[...] = e / jnp.sum(e, -1, keepdims=True)


def _post_mixer(m, x, w_out, g, b, rw, rb):
    n = m.shape[0]
    tm = _row_tile(n, (256, 128, 64, 32, 16))
    row = pl.BlockSpec((tm, D_MODEL), lambda i: (i, 0))
    vec = pl.BlockSpec((1, D_MODEL), lambda i: (0, 0))
    small = pl.BlockSpec((tm, LANE), lambda i: (i, 0))
    return pl.pallas_call(
        _post_mixer_kernel,
        grid=(n // tm,),
        in_specs=[row, row, pl.BlockSpec((D_MODEL, D_MODEL), lambda i: (0, 0)), vec, vec,
                  pl.BlockSpec((D_MODEL, LANE), lambda i: (0, 0)), pl.BlockSpec((1, LANE), lambda i: (0, 0))],
        out_specs=[row, small, small],
        out_shape=[jax.ShapeDtypeStruct((n, D_MODEL), F32),
                   jax.ShapeDtypeStruct((n, LANE), jnp.int32), jax.ShapeDtypeStruct((n, LANE), F32)],
        compiler_params=_cparams(1),
        name="post_mixer",
    )(m, x, w_out, g, b, rw, rb)


def _moe_kernel(be_ref, nu_ref, x_ref, wg_ref, wl_ref, bg_ref, bl_ref, wd_ref, bd_ref, o_ref, xb_ref):
    i = pl.program_id(0)
    t = pl.program_id(1)

    @pl.when((i < nu_ref[0]) & (t == 0))
    def _():
        xb_ref[...] = x_ref[...].astype(BF16)

    @pl.when(i < nu_ref[0])
    def _():
        x = xb_ref[...]
        hg = jnp.dot(x, wg_ref[...], preferred_element_type=F32) + bg_ref[...]
        hl = jnp.dot(x, wl_ref[...], preferred_element_type=F32) + bl_ref[...]
        glu = jnp.minimum(hg, SWIGLU_LIMIT)
        lin = jnp.clip(hl, -SWIGLU_LIMIT, SWIGLU_LIMIT)
        act = glu * _sigmoid(SWIGLU_ALPHA * glu) * (lin + 1.0)
        part = jnp.dot(act.astype(BF16), wd_ref[...].astype(BF16), preferred_element_type=F32)

        @pl.when(t == 0)
        def _():
            o_ref[...] = part + bd_ref[...]

        @pl.when(t > 0)
        def _():
            o_ref[...] += part


def _moe(x_pad, blk_e, n_used, wg, wl, bg, bl, wd, bd):
    npad = x_pad.shape[0]
    nb = npad // MOE_BM
    nh = D_EXPERT // MOE_TH

    def blk(i, nu):
        return jnp.minimum(i, nu[0] - 1)

    def tt(i, t, nu):
        return jnp.where(i < nu[0], t, nh - 1)
    return pl.pallas_call(
        _moe_kernel,
        grid_spec=pltpu.PrefetchScalarGridSpec(
            num_scalar_prefetch=2,
            grid=(nb, nh),
            in_specs=[
                pl.BlockSpec((MOE_BM, D_MODEL), lambda i, t, be, nu: (blk(i, nu), 0)),
                pl.BlockSpec((None, D_MODEL, MOE_TH), lambda i, t, be, nu: (be[blk(i, nu)], 0, tt(i, t, nu))),
                pl.BlockSpec((None, D_MODEL, MOE_TH), lambda i, t, be, nu: (be[blk(i, nu)], 0, tt(i, t, nu))),
                pl.BlockSpec((None, 1, MOE_TH), lambda i, t, be, nu: (be[blk(i, nu)], 0, tt(i, t, nu))),
                pl.BlockSpec((None, 1, MOE_TH), lambda i, t, be, nu: (be[blk(i, nu)], 0, tt(i, t, nu))),
                pl.BlockSpec((None, MOE_TH, D_MODEL), lambda i, t, be, nu: (be[blk(i, nu)], tt(i, t, nu), 0)),
                pl.BlockSpec((None, 1, D_MODEL), lambda i, t, be, nu: (be[blk(i, nu)], 0, 0)),
            ],
            out_specs=pl.BlockSpec((MOE_BM, D_MODEL), lambda i, t, be, nu: (blk(i, nu), 0)),
            scratch_shapes=[pltpu.VMEM((MOE_BM, D_MODEL), BF16)],
        ),
        out_shape=jax.ShapeDtypeStruct((npad, D_MODEL), F32),
        compiler_params=_cparams(2),
        name="moe",
    )(blk_e, n_used, x_pad, wg, wl, bg, bl, wd, bd)


def _route(top_e, n):
    nk = n * TOP_K
    e_flat = top_e.reshape(-1)
    onehot = (e_flat[:, None] == jnp.arange(N_EXPERTS, dtype=jnp.int32)[None, :]).astype(jnp.int32)
    rank = jnp.take_along_axis(jnp.cumsum(onehot, 0), e_flat[:, None], 1)[:, 0] - 1
    counts = jnp.sum(onehot, 0)
    padded = (counts + MOE_BM - 1) // MOE_BM * MOE_BM
    pend = jnp.cumsum(padded)
    pstart = pend - padded
    dest = pstart[e_flat] + rank
    npad = -(-(nk + N_EXPERTS * (MOE_BM - 1)) // MOE_BM) * MOE_BM
    nb = npad // MOE_BM
    tok = jnp.repeat(jnp.arange(n, dtype=jnp.int32), TOP_K)
    tok_pad = jnp.zeros((npad,), jnp.int32).at[dest].set(tok)
    blk_e = jnp.minimum(jnp.searchsorted(pend, jnp.arange(nb, dtype=jnp.int32) * MOE_BM, side='right'),
                        N_EXPERTS - 1).astype(jnp.int32)
    n_used = (pend[-1] // MOE_BM).astype(jnp.int32).reshape(1)
    return dest.reshape(n, TOP_K), tok_pad, blk_e, n_used


def _post_moe_kernel(x_ref, y_ref, w_ref, g_ref, b_ref, o_ref):
    w = w_ref[...]
    moe = w[:, 0:1] * y_ref[0]
    for k in range(1, TOP_K):
        moe = moe + w[:, k:k + 1] * y_ref[k]
    o_ref[...] = _layer_norm(DEEPNORM_ALPHA * x_ref[...] + moe, g_ref[...], b_ref[...])


def _post_moe(x1, yg, tw, g, b):
    n = x1.shape[0]
    tm = _row_tile(n, (256, 128, 64, 32, 16))
    row = pl.BlockSpec((tm, D_MODEL), lambda i: (i, 0))
    vec = pl.BlockSpec((1, D_MODEL), lambda i: (0, 0))
    return pl.pallas_call(
        _post_moe_kernel,
        grid=(n // tm,),
        in_specs=[row, pl.BlockSpec((TOP_K, tm, D_MODEL), lambda i: (0, i, 0)),
                  pl.BlockSpec((tm, LANE), lambda i: (i, 0)), vec, vec],
        out_specs=row,
        out_shape=jax.ShapeDtypeStruct((n, D_MODEL), F32),
        compiler_params=_cparams(1),
        name="post_moe",
    )(x1, yg, tw, g, b)


def kernel(x_prompt, x_sample, cache_k_cmp, cache_v_cmp, cache_k_sel, cache_v_sel, state_win_k, state_win_v, state_conv, page_table, w_in, b_in, cmp_pe, cmp_w1, cmp_w2, w_oa, dw_w, dw_b, cn_g, cn_b, w_pb, w_out, ln1_g, ln1_b, router_w, router_b, exp_w_up, exp_b_up, exp_w_down, exp_b_down, ln2_g, ln2_b):
    b_, t, _ = x_prompt.shape
    db, s_len, _ = x_sample.shape
    depth = w_in.shape[0]
    n_pages = page_table.shape[1]
    past = n_pages * cache_k_cmp.shape[2]
    wb = state_win_k.shape[2]
    n_p = b_ * t
    n_s = db * s_len
    n = n_p + n_s
    nw = min(WINDOW, t)

    x = jnp.concatenate([x_prompt.reshape(n_p, D_MODEL), x_sample.reshape(n_s, D_MODEL)], 0)
    pos = jnp.concatenate([jnp.tile(jnp.arange(t, dtype=jnp.int32), b_),
                           jnp.tile(past + jnp.arange(s_len, dtype=jnp.int32), db)])
    cos2, sin2 = _rope_tables(pos)
    nc_p = t // CMP_STRIDE
    nc_s = n_pages * PAGE_CHUNKS
    cmp_cos_p, cmp_sin_p = _rope_tables(jnp.arange(nc_p, dtype=jnp.int32) * CMP_STRIDE + CMP_BLOCK - 1)
    cmp_cos_s, cmp_sin_s = _rope_tables(
        jnp.repeat(jnp.arange(nc_s, dtype=jnp.int32), N_GROUPS) * CMP_STRIDE + CMP_BLOCK - 1)

    def pages(c):
        return c.reshape(c.shape[0], c.shape[1], PAGE_FLAT, HEAD_DIM)
    ck_cmp, cv_cmp, ck_sel, cv_sel = pages(cache_k_cmp), pages(cache_v_cmp), pages(cache_k_sel), pages(cache_v_sel)
    swk_all = state_win_k.reshape(depth, db, wb * N_GROUPS, HEAD_DIM)
    swv_all = state_win_v.reshape(depth, db, wb * N_GROUPS, HEAD_DIM)
    conv_zero = jnp.zeros((b_, CONV_PAD, CONV_DIM), F32)
    conv_state = jnp.pad(state_conv, ((0, 0), (0, 0), (CONV_PAD - (CONV_W - 1), 0), (0, 0)))

    outs = [[] for _ in range(14)]
    for l in range(depth):
        w_in_r, b_in_r = _prep_w_in(w_in[l], b_in[l])
        z = _inproj(x, w_in_r, b_in_r, cos2, sin2)

        w1 = cmp_w1[l].reshape(2, CMP_BLOCK * HEAD_DIM, CMP_HIDDEN).astype(BF16)
        w01 = jnp.concatenate([w1[:, :CHUNK_K], w1[:, CHUNK_K:]], -1)
        pe8 = jnp.broadcast_to(cmp_pe[l].reshape(2, 1, CMP_BLOCK * HEAD_DIM), (2, 8, CMP_BLOCK * HEAD_DIM)).astype(BF16)
        w2 = cmp_w2[l].astype(BF16)
        ab_p = _cmp1_prompt(z, b_, t, w01).reshape(2, b_ * N_GROUPS, nc_p, 2 * CMP_HIDDEN)
        kc_p, vc_p = _cmp2(ab_p[0], ab_p[1], pe8[0], pe8[1], w1[0], w1[1], w2[0], w2[1], cmp_cos_p, cmp_sin_p, 1)
        abk_s, abv_s = _cmp1_sample(ck_cmp, cv_cmp, page_table, l, w01[0], w01[1])
        kc_s, vc_s = _cmp2(abk_s, abv_s, pe8[0], pe8[1], w1[0], w1[1], w2[0], w2[1], cmp_cos_s, cmp_sin_s,
                           N_GROUPS)

        o_p = _attn_prompt(z, kc_p, vc_p, b_, t)
        oc_s, sel_s = _attn_sample_cmp(z, n_p, kc_s, vc_s, db, s_len, past)
        o_s = _attn_sample(z, n_p, ck_sel, cv_sel, page_table, l, swk_all, swv_all, oc_s, sel_s, s_len, past)
        o = jnp.concatenate([o_p, o_s.astype(BF16)], 0)

        y_p, cb_p = _conv(z, 0, b_, t, conv_zero, dw_w[l], dw_b[l], cn_g[l], cn_b[l])
        y_s, cb_s = _conv(z, n_p, db, s_len, conv_state[l], dw_w[l], dw_b[l], cn_g[l], cn_b[l])
        y = jnp.concatenate([y_p, y_s.astype(BF16)], 0)

        m = _merge(o, y, w_oa[l].astype(BF16), w_pb[l].astype(BF16), z)
        rw = jnp.pad(router_w[l], ((0, 0), (0, LANE - N_EXPERTS)))
        rb = jnp.pad(router_b[l], (0, LANE - N_EXPERTS))[None]
        x1, te, tw = _post_mixer(m, x, w_out[l].astype(BF16), ln1_g[l][None], ln1_b[l][None], rw, rb)

        dest, tok_pad, blk_e, n_used = _route(te[:, :TOP_K], n)
        wu = exp_w_up[l].reshape(N_EXPERTS, D_MODEL, D_EXPERT, 2)
        bu = exp_b_up[l].reshape(N_EXPERTS, 1, D_EXPERT, 2)
        y_pad = _moe(x1[tok_pad], blk_e, n_used, wu[..., 0].astype(BF16), wu[..., 1].astype(BF16),
                     bu[..., 0], bu[..., 1], exp_w_down[l], exp_b_down[l][:, None, :])
        x = _post_moe(x1, y_pad[dest.T], tw, ln2_g[l][None], ln2_b[l][None])

        def zp(col):
            return z[:n_p, col:col + KV_DIM].reshape(b_, t, N_GROUPS, HEAD_DIM)

        def zs(col):
            return z[n_p:, col:col + KV_DIM].reshape(db, s_len, N_GROUPS, HEAD_DIM)
        vals = [zp(COL_KC), zp(COL_VC), zp(COL_KS), zp(COL_VS), zp(COL_KW)[:, t - nw:], zp(COL_VW)[:, t - nw:],
                cb_p[:, CONV_PAD - (CONV_W - 1):],
                zs(COL_KC), zs(COL_VC), zs(COL_KS), zs(COL_VS),
                jnp.concatenate([state_win_k[l], zs(COL_KW)], 1)[:, s_len:],
                jnp.concatenate([state_win_v[l], zs(COL_VW)], 1)[:, s_len:],
                cb_s[:, CONV_PAD - (CONV_W - 1):]]
        for lst, v in zip(outs, vals):
            lst.append(v)

    return (x[:n_p].reshape(b_, t, D_MODEL), x[n_p:].reshape(db, s_len, D_MODEL),
            *[jnp.stack(v) for v in outs])
```

```python
import functools
import math

import jax
import jax.numpy as jnp
from jax import lax
from jax.experimental import pallas as pl
from jax.experimental.pallas import tpu as pltpu

F32 = jnp.float32
BF16 = jnp.bfloat16

D_MODEL = 2048
N_HEADS = 16
HEAD_DIM = 128
N_GROUPS = 4
GROUP_HEADS = N_HEADS // N_GROUPS
GROUP_DIM = GROUP_HEADS * HEAD_DIM
KV_DIM = N_GROUPS * HEAD_DIM
CMP_BLOCK = 32
CMP_STRIDE = 16
CMP_HIDDEN = 256
SEL_BLOCK = 64
SEL_TOP = 16
WINDOW = 512
Q_BLOCK = 128
CONV_DIM = D_MODEL // 2
CONV_W = 31
CONV_PAD = 32
N_EXPERTS = 32
TOP_K = 4
D_EXPERT = D_MODEL
SWIGLU_LIMIT = 7.0
SWIGLU_ALPHA = 1.702
ROPE_THETA = 10000.0
LN_EPS = 1e-5
NEG_INF = -1e30
REMOVED = -3e38
FORCE_SCORE = 1e9
DEPTH = 2
DEEPNORM_ALPHA = (2 * DEPTH) ** 0.25
ATTN_SCALE = HEAD_DIM ** -0.5
GATE_DIM = 3 * N_HEADS

LANE = 128
TN_IN = 512
COL_Q = 0
COL_KS = 2048
COL_KW = 2560
COL_KC = 3072
COL_VC = 3584
COL_VS = 4096
COL_VW = 4608
COL_U = 5120
COL_GA = 7168
COL_GB = 9216
COL_GT = 11264
P_PAD = 11776
N_ROPE_TILES = COL_KC // TN_IN
N_PLAIN_TILES = (COL_GA - COL_KC) // TN_IN

MOE_BM = 512
MOE_TH = 1024
MOE_SUB = 256
SEL_TK = 512
PAGES_PER_STEP = 8
CMP_PAGES_PER_STEP = 16
DEINT_TM = 512
DEINT_GROUP = 256
VMEM_LIMIT = 56 * 1024 * 1024


def _cparams(n_axes):
    return pltpu.CompilerParams(dimension_semantics=("arbitrary",) * n_axes,
                                vmem_limit_bytes=VMEM_LIMIT)


def _row_tile(n, cands):
    for c in cands:
        if n % c == 0:
            return c
    raise ValueError(f"no row tile for {n}")


def _sigmoid(x):
    return 1.0 / (1.0 + jnp.exp(-x))


def _layer_norm(x, g, b):
    mu = jnp.mean(x, -1, keepdims=True)
    xc = x - mu
    var = jnp.mean(xc * xc, -1, keepdims=True)
    return xc * lax.rsqrt(var + LN_EPS) * g + b


def _rope_apply(z, cos2, sin2):
    return z * cos2 + pltpu.roll(z, HEAD_DIM // 2, 1) * sin2


def _rope_tables(pos):
    half = HEAD_DIM // 2
    inv = ROPE_THETA ** (-jnp.arange(half, dtype=F32) / half)
    ang = pos.astype(F32)[:, None] * inv[None, :]
    c, s = jnp.cos(ang), jnp.sin(ang)
    return jnp.concatenate([c, c], -1), jnp.concatenate([-s, s], -1)


def _dot_nt(a, b):
    return lax.dot_general(a, b, (((1,), (1,)), ((), ())), preferred_element_type=F32)


def _inproj_kernel(x_ref, w_ref, b_ref, cos_ref, sin_ref, o_ref, xb_ref):
    j = pl.program_id(1)

    @pl.when(j == 0)
    def _():
        xb_ref[...] = x_ref[...].astype(BF16)

    z = jnp.dot(xb_ref[...], w_ref[...], preferred_element_type=F32) + b_ref[...]

    @pl.when(j < N_ROPE_TILES)
    def _():
        c = cos_ref[...]
        s = sin_ref[...]
        for h in range(TN_IN // HEAD_DIM):
            sl = slice(h * HEAD_DIM, (h + 1) * HEAD_DIM)
            o_ref[:, sl] = _rope_apply(z[:, sl], c, s)

    @pl.when((j >= N_ROPE_TILES) & (j < N_ROPE_TILES + N_PLAIN_TILES))
    def _():
        o_ref[...] = z

    @pl.when(j >= N_ROPE_TILES + N_PLAIN_TILES)
    def _():
        o_ref[...] = _sigmoid(z)


def _inproj(x, w, b, cos2, sin2):
    n = x.shape[0]
    tm = _row_tile(n, (768, 512, 256, 128, 64, 32, 16, 8))
    return pl.pallas_call(
        _inproj_kernel,
        grid=(n // tm, P_PAD // TN_IN),
        in_specs=[
            pl.BlockSpec((tm, D_MODEL), lambda i, j: (i, 0)),
            pl.BlockSpec((D_MODEL, TN_IN), lambda i, j: (0, j)),
            pl.BlockSpec((1, TN_IN), lambda i, j: (0, j)),
            pl.BlockSpec((tm, HEAD_DIM), lambda i, j: (i, 0)),
            pl.BlockSpec((tm, HEAD_DIM), lambda i, j: (i, 0)),
        ],
        out_specs=pl.BlockSpec((tm, TN_IN), lambda i, j: (i, j)),
        out_shape=jax.ShapeDtypeStruct((n, P_PAD), F32),
        scratch_shapes=[pltpu.VMEM((tm, D_MODEL), BF16)],
        compiler_params=_cparams(2),
        name="inproj",
    )(x, w, b, cos2, sin2)


def _prep_w_in(w_in, b_in):
    o_q, o_kc, o_vc, o_ks, o_vs, o_kw, o_vw = (0, 2048, 2560, 3072, 3584, 4096, 4608)
    o_gt = 5120
    o_u = o_gt + GATE_DIM
    o_gbr = o_u + 2 * CONV_DIM

    def build(a):
        def sl(o, n):
            return a[..., o:o + n]
        gates = sl(o_gt, GATE_DIM).reshape(a.shape[:-1] + (N_GROUPS, 3 * GROUP_HEADS))
        pad = [(0, 0)] * (gates.ndim - 1) + [(0, LANE - 3 * GROUP_HEADS)]
        gates = jnp.pad(gates, pad).reshape(a.shape[:-1] + (N_GROUPS * LANE,))
        return jnp.concatenate([
            sl(o_q, 2048), sl(o_ks, KV_DIM), sl(o_kw, KV_DIM),
            sl(o_kc, KV_DIM), sl(o_vc, KV_DIM), sl(o_vs, KV_DIM), sl(o_vw, KV_DIM), sl(o_u, 2 * CONV_DIM),
            sl(o_gbr, 2 * D_MODEL), gates], -1)

    return build(w_in).astype(BF16), build(b_in)[None, :]


CONV_RC = 32


def _conv_kernel(*refs, tt, has_prev):
    if has_prev:
        ua_ref, ub_ref, pa_ref, pb_ref, buf_ref, w_ref, b_ref, g_ref, be_ref, y_ref, nb_ref, hx_ref = refs
    else:
        ua_ref, ub_ref, buf_ref, w_ref, b_ref, g_ref, be_ref, y_ref, nb_ref, hx_ref = refs
    i = pl.program_id(1)
    hx_ref[CONV_PAD:CONV_PAD + tt, :] = ua_ref[...] * _sigmoid(ub_ref[...])
    if has_prev:
        @pl.when(i == 0)
        def _():
            hx_ref[0:CONV_PAD, :] = buf_ref[...]

        @pl.when(i > 0)
        def _():
            hx_ref[0:CONV_PAD, :] = pa_ref[...] * _sigmoid(pb_ref[...])
    else:
        hx_ref[0:CONV_PAD, :] = buf_ref[...]
    rc = min(CONV_RC, tt)
    off = CONV_PAD - (CONV_W - 1)
    for c in range(tt // rc):
        acc = jnp.zeros((rc, CONV_DIM), F32) + b_ref[...]
        for k in range(CONV_W):
            acc = acc + hx_ref[c * rc + off + k:c * rc + off + k + rc, :] * w_ref[k:k + 1, :]
        yn = _layer_norm(acc, g_ref[...], be_ref[...])
        y_ref[c * rc:(c + 1) * rc, :] = (yn * _sigmoid(yn)).astype(y_ref.dtype)
    nb_ref[...] = hx_ref[tt:tt + CONV_PAD, :]


def _conv(z, row0, nb_, t, buf, dw_w, dw_b, cn_g, cn_b):
    tt = min(t, 128)
    nt = t // tt
    has_prev = nt > 1
    rb0 = row0 // tt
    ca, cb = COL_U // CONV_DIM, COL_U // CONV_DIM + 1
    in_specs = [
        pl.BlockSpec((tt, CONV_DIM), lambda b, i: (rb0 + b * nt + i, ca)),
        pl.BlockSpec((tt, CONV_DIM), lambda b, i: (rb0 + b * nt + i, cb)),
    ]
    args = [z, z]
    if has_prev:
        pr = tt // CONV_PAD
        pb0 = row0 // CONV_PAD

        def prev_idx(b, i):
            return jnp.maximum(pb0 + (b * nt + i) * pr - 1, 0)
        in_specs += [
            pl.BlockSpec((CONV_PAD, CONV_DIM), lambda b, i: (prev_idx(b, i), ca)),
            pl.BlockSpec((CONV_PAD, CONV_DIM), lambda b, i: (prev_idx(b, i), cb)),
        ]
        args += [z, z]
    in_specs += [
        pl.BlockSpec((None, CONV_PAD, CONV_DIM), lambda b, i: (b, 0, 0)),
        pl.BlockSpec((CONV_PAD, CONV_DIM), lambda b, i: (0, 0)),
        pl.BlockSpec((1, CONV_DIM), lambda b, i: (0, 0)),
        pl.BlockSpec((1, CONV_DIM), lambda b, i: (0, 0)),
        pl.BlockSpec((1, CONV_DIM), lambda b, i: (0, 0)),
    ]
    w_pad = jnp.pad(dw_w, ((0, CONV_PAD - CONV_W), (0, 0)))
    args += [buf, w_pad, dw_b[None], cn_g[None], cn_b[None]]
    return pl.pallas_call(
        functools.partial(_conv_kernel, tt=tt, has_prev=has_prev),
        grid=(nb_, nt),
        in_specs=in_specs,
        out_specs=[
            pl.BlockSpec((tt, CONV_DIM), lambda b, i: (b * nt + i, 0)),
            pl.BlockSpec((None, CONV_PAD, CONV_DIM), lambda b, i: (b, 0, 0)),
        ],
        out_shape=[jax.ShapeDtypeStruct((nb_ * t, CONV_DIM), BF16 if tt % 16 == 0 else F32),
                   jax.ShapeDtypeStruct((nb_, CONV_PAD, CONV_DIM), F32)],
        scratch_shapes=[pltpu.VMEM((tt + CONV_PAD, CONV_DIM), F32)],
        compiler_params=_cparams(2),
        name="conv",
    )(*args)


CHUNK_K = CMP_STRIDE * HEAD_DIM


def _cmp1_prompt_kernel(*refs, nc):
    srcs = refs[:N_GROUPS]
    w_ref, out_ref, c_ref = refs[N_GROUPS:]
    for g in range(N_GROUPS):
        for s in range(CMP_STRIDE):
            c_ref[g * nc:(g + 1) * nc, s * HEAD_DIM:(s + 1) * HEAD_DIM] = (
                srcs[g][pl.ds(s, nc, stride=CMP_STRIDE), :])
    ab = jnp.dot(c_ref[...].astype(BF16), w_ref[...], preferred_element_type=F32)
    for g in range(N_GROUPS):
        out_ref[g] = ab[g * nc:(g + 1) * nc, :]


def _cmp1_prompt(z, b_, t, w01):
    nc = t // CMP_STRIDE
    assert COL_VC == COL_KC + KV_DIM
    return pl.pallas_call(
        functools.partial(_cmp1_prompt_kernel, nc=nc),
        grid=(2, b_),
        in_specs=[pl.BlockSpec((t, HEAD_DIM), functools.partial(
            lambda kv, b, g: (b, COL_KC // HEAD_DIM + kv * N_GROUPS + g), g=g)) for g in range(N_GROUPS)] + [
            pl.BlockSpec((None, CHUNK_K, 2 * CMP_HIDDEN), lambda kv, b: (kv, 0, 0)),
        ],
        out_specs=pl.BlockSpec((None, None, N_GROUPS, nc, 2 * CMP_HIDDEN), lambda kv, b: (kv, b, 0, 0, 0)),
        out_shape=jax.ShapeDtypeStruct((2, b_, N_GROUPS, nc, 2 * CMP_HIDDEN), F32),
        scratch_shapes=[pltpu.VMEM((N_GROUPS * nc, CHUNK_K), F32)],
        compiler_params=_cparams(2),
        name="cmp1_prompt",
    )(*([z] * N_GROUPS), w01)


PAGE_ROWS = 128
PAGE_FLAT = PAGE_ROWS * N_GROUPS
PAGE_CHUNKS = PAGE_ROWS // CMP_STRIDE


def _cmp1_sample_kernel(pt_ref, *refs, npg):
    k_pages = refs[:npg]
    v_pages = refs[npg:2 * npg]
    wk_ref, wv_ref, abk_ref, abv_ref, c_ref = refs[2 * npg:]
    low = lax.broadcasted_iota(jnp.int32, (2 * N_GROUPS, HEAD_DIM), 0) < N_GROUPS
    half = CMP_STRIDE // 2
    for pages, w_ref, out_ref in ((k_pages, wk_ref, abk_ref), (v_pages, wv_ref, abv_ref)):
        for p in range(npg):
            for n2 in range(PAGE_CHUNKS // 2):
                r0 = (p * PAGE_CHUNKS + 2 * n2) * N_GROUPS
                for sp in range(half):
                    ja = (2 * n2 * half + sp) * 8
                    jb = ((2 * n2 + 1) * half + sp) * 8
                    a = pages[p][ja:ja + 8, :]
                    b = pages[p][jb:jb + 8, :]
                    c_ref[r0:r0 + 8, (2 * sp) * HEAD_DIM:(2 * sp + 1) * HEAD_DIM] = (
                        jnp.where(low, a, pltpu.roll(b, N_GROUPS, 0)))
                    c_ref[r0:r0 + 8, (2 * sp + 1) * HEAD_DIM:(2 * sp + 2) * HEAD_DIM] = (
                        jnp.where(low, pltpu.roll(a, N_GROUPS, 0), b))
        out_ref[...] = jnp.dot(c_ref[...].astype(BF16), w_ref[...], preferred_element_type=F32)


def _page_spec(layer, p, npg):
    return pl.BlockSpec((None, None, PAGE_FLAT, HEAD_DIM),
                        lambda b, c, pt: (layer, pt[b, c * npg + p], 0, 0))


def _cmp1_sample(cache_k, cache_v, page_table, layer, wk, wv):
    assert 2 * N_GROUPS == 8 and CMP_STRIDE % 2 == 0
    db, n_pages = page_table.shape
    npg = math.gcd(n_pages, CMP_PAGES_PER_STEP)
    rows = npg * PAGE_CHUNKS * N_GROUPS
    shp = jax.ShapeDtypeStruct((db, n_pages * PAGE_CHUNKS * N_GROUPS, 2 * CMP_HIDDEN), F32)
    ospec = pl.BlockSpec((None, rows, 2 * CMP_HIDDEN), lambda b, c, pt: (b, c, 0))
    wspec = pl.BlockSpec((CHUNK_K, 2 * CMP_HIDDEN), lambda b, c, pt: (0, 0))
    return pl.pallas_call(
        functools.partial(_cmp1_sample_kernel, npg=npg),
        grid_spec=pltpu.PrefetchScalarGridSpec(
            num_scalar_prefetch=1,
            grid=(db, n_pages // npg),
            in_specs=[_page_spec(layer, p, npg) for p in range(npg)] * 2 + [wspec, wspec],
            out_specs=[ospec, ospec],
            scratch_shapes=[pltpu.VMEM((rows, CHUNK_K), F32)],
        ),
        out_shape=[shp, shp],
        compiler_params=_cparams(2),
        name="cmp1_sample",
    )(page_table, *([cache_k] * npg), *([cache_v] * npg), wk, wv)


def _cmp2_kernel(abk_ref, abv_ref, pek_ref, pev_ref, w1k_ref, w1v_ref, w2k_ref, w2v_ref,
                 cos_ref, sin_ref, kc_ref, vc_ref, *, shift):
    def one(ab_ref, pe_ref, w1_ref, w2_ref):
        ab = ab_ref[...]
        pe_term = jnp.dot(pe_ref[...], w1_ref[...], preferred_element_type=F32)[0:1, :]
        h = ab[:, :CMP_HIDDEN] + pltpu.roll(ab[:, CMP_HIDDEN:], ab.shape[0] - shift, 0) + pe_term
        return jnp.dot(jax.nn.gelu(h).astype(BF16), w2_ref[...], preferred_element_type=F32)

    kc_ref[...] = _rope_apply(one(abk_ref, pek_ref, w1k_ref, w2k_ref), cos_ref[...], sin_ref[...])
    vc_ref[...] = one(abv_ref, pev_ref, w1v_ref, w2v_ref)


def _cmp2(abk, abv, pek, pev, w1k, w1v, w2k, w2v, cos2, sin2, shift):
    nb_, rows, _ = abk.shape
    abspec = pl.BlockSpec((None, rows, 2 * CMP_HIDDEN), lambda b: (b, 0, 0))
    ospec = pl.BlockSpec((None, rows, HEAD_DIM), lambda b: (b, 0, 0))
    shp = jax.ShapeDtypeStruct((nb_, rows, HEAD_DIM), F32)

    def full(a):
        return pl.BlockSpec(a.shape, lambda b: (0,) * a.ndim)
    return pl.pallas_call(
        functools.partial(_cmp2_kernel, shift=shift),
        grid=(nb_,),
        in_specs=[abspec, abspec, full(pek), full(pev), full(w1k), full(w1v), full(w2k), full(w2v),
                  full(cos2), full(sin2)],
        out_specs=[ospec, ospec],
        out_shape=[shp, shp],
        compiler_params=_cparams(1),
        name="cmp2",
    )(abk, abv, pek, pev, w1k, w1v, w2k, w2v, cos2, sin2)


def _masked_softmax(s, maskf):
    m = jnp.max(s, -1, keepdims=True)
    p = jnp.exp(s - m) * maskf
    return p / jnp.maximum(jnp.sum(p, -1, keepdims=True), 1e-30)


def _bias(maskf):
    return (maskf - 1.0) * (-NEG_INF)


def _split_dot(p, mat_bf16):
    hi = p.astype(BF16)
    lo = (p - hi.astype(F32)).astype(BF16)
    return (jnp.dot(hi, mat_bf16, preferred_element_type=F32)
            + jnp.dot(lo, mat_bf16, preferred_element_type=F32))


def _cmp_to_sel(n_rows, n_cols):
    cs = lax.broadcasted_iota(jnp.int32, (n_rows, n_cols), 0) * CMP_STRIDE
    ss = lax.broadcasted_iota(jnp.int32, (n_rows, n_cols), 1) * SEL_BLOCK
    return jnp.where((cs < ss + SEL_BLOCK) & (cs + CMP_BLOCK > ss), 1.0, 0.0).astype(BF16)


def _cmp_to_sel_t(n_rows, n_cols):
    ss = lax.broadcasted_iota(jnp.int32, (n_rows, n_cols), 0) * SEL_BLOCK
    cs = lax.broadcasted_iota(jnp.int32, (n_rows, n_cols), 1) * CMP_STRIDE
    return jnp.where((cs < ss + SEL_BLOCK) & (cs + CMP_BLOCK > ss), 1.0, 0.0).astype(BF16)


def _select_blocks(imp, tq, n_sel):
    blk = lax.broadcasted_iota(jnp.int32, imp.shape, 1)
    cur = tq // SEL_BLOCK
    forced = (blk == 0) | (blk == cur) | (blk == cur - 1)
    future = blk * SEL_BLOCK > tq
    work = jnp.where(future, NEG_INF, jnp.where(forced, FORCE_SCORE, imp))
    work = jnp.where(blk < n_sel, work, REMOVED)
    sel = jnp.zeros(imp.shape, F32)
    for _ in range(SEL_TOP):
        mx = jnp.max(work, -1, keepdims=True)
        first = jnp.min(jnp.where(work == mx, blk, imp.shape[1]), -1, keepdims=True)
        pick = blk == first
        sel = jnp.where(pick, 1.0, sel)
        work = jnp.where(pick, REMOVED, work)
    return jnp.where(future, 0.0, sel)


def _select_blocks_t(imp_t, t0, n_top):
    n_sel = imp_t.shape[0]
    blk = lax.broadcasted_iota(jnp.int32, imp_t.shape, 0)
    tq = t0 + lax.broadcasted_iota(jnp.int32, imp_t.shape, 1)
    cur = tq // SEL_BLOCK
    forced = (blk == 0) | (blk == cur) | (blk == cur - 1)
    future = blk * SEL_BLOCK > tq
    work = jnp.where(future, NEG_INF, jnp.where(forced, FORCE_SCORE, imp_t))
    cnt = jnp.zeros(imp_t.shape, F32)
    for k in range(n_sel):
        wk = work[k:k + 1, :]
        ge = jnp.where(wk >= work, 1.0, 0.0)
        gt = jnp.where(wk > work, 1.0, 0.0)
        cnt = cnt + jnp.where(blk > k, ge, gt)
    return jnp.where(future, 0.0, jnp.where(cnt < n_top, 1.0, 0.0))


def _attn_prompt_kernel(q_ref, ks_ref, vs_ref, kw_ref, vw_ref, kc_ref, vc_ref, gt_ref, e_ref, o_ref, *, t_len, nc):
    i = pl.program_id(2)
    t0 = pl.multiple_of(i * Q_BLOCK, Q_BLOCK)
    qf = q_ref[...] * ATTN_SCALE
    q = jnp.concatenate([qf[:, r * HEAD_DIM:(r + 1) * HEAD_DIM] for r in range(GROUP_HEADS)], 0).astype(BF16)
    tq = t0 + lax.broadcasted_iota(jnp.int32, (Q_BLOCK, 1), 0)

    def per_head(a):
        return jnp.concatenate([a] * GROUP_HEADS, 0)

    ends = lax.broadcasted_iota(jnp.int32, (Q_BLOCK, nc), 1) * CMP_STRIDE + (CMP_BLOCK - 1)
    mc = per_head(jnp.where(ends <= tq, 1.0, 0.0))
    s_c = _dot_nt(q, kc_ref[...].astype(BF16)) + _bias(mc)
    p_c = _masked_softmax(s_c, mc)
    o_c = jnp.dot(p_c.astype(BF16), vc_ref[...].astype(BF16), preferred_element_type=F32)
    p_sum = p_c[0:Q_BLOCK]
    for r in range(1, GROUP_HEADS):
        p_sum = p_sum + p_c[r * Q_BLOCK:(r + 1) * Q_BLOCK]
    n_sel = t_len // SEL_BLOCK
    m2s_t = _cmp_to_sel_t(LANE, nc)
    p_hi = p_sum.astype(BF16)
    p_lo = (p_sum - p_hi.astype(F32)).astype(BF16)
    imp_t = _dot_nt(m2s_t, p_hi) + _dot_nt(m2s_t, p_lo)
    sel_t = _select_blocks_t(imp_t[:n_sel], t0, SEL_TOP)
    if n_sel < LANE:
        sel_t = jnp.concatenate([sel_t, jnp.zeros((LANE - n_sel, Q_BLOCK), F32)], 0)
    sel = jnp.transpose(sel_t)

    blk_lane = lax.broadcasted_iota(jnp.int32, (Q_BLOCK, LANE), 1)
    sel_bias = jnp.where(blk_lane < t0 // SEL_BLOCK, (sel - 1.0) * (-NEG_INF), NEG_INF)
    qa = jnp.concatenate([q, per_head(sel_bias.astype(BF16))], 1)
    kj = lax.broadcasted_iota(jnp.int32, (Q_BLOCK, Q_BLOCK), 1)
    qi = lax.broadcasted_iota(jnp.int32, (Q_BLOCK, Q_BLOCK), 0)
    causal = per_head(jnp.where(kj <= qi, 0.0, NEG_INF))
    s_d = _dot_nt(q, ks_ref[pl.ds(t0, Q_BLOCK), :].astype(BF16)) + causal
    m_d = jnp.max(s_d, -1, keepdims=True)
    p_d = jnp.exp(s_d - m_d)
    init = (m_d, jnp.sum(p_d, -1, keepdims=True),
            jnp.dot(p_d.astype(BF16), vs_ref[pl.ds(t0, Q_BLOCK), :].astype(BF16), preferred_element_type=F32))

    def sel_tile(kt, carry):
        m_i, l_i, acc = carry
        k0 = pl.multiple_of(kt * SEL_TK, SEL_TK)
        ka = jnp.concatenate([ks_ref[pl.ds(k0, SEL_TK), :].astype(BF16), e_ref[pl.ds(k0, SEL_TK), :]], 1)
        s = _dot_nt(qa, ka)
        m_new = jnp.maximum(m_i, jnp.max(s, -1, keepdims=True))
        alpha = jnp.exp(m_i - m_new)
        p = jnp.exp(s - m_new)
        l_new = alpha * l_i + jnp.sum(p, -1, keepdims=True)
        acc_new = alpha * acc + jnp.dot(p.astype(BF16), vs_ref[pl.ds(k0, SEL_TK), :].astype(BF16),
                                        preferred_element_type=F32)
        return m_new, l_new, acc_new

    _, l_s, acc_s = lax.fori_loop(0, (t0 + SEL_TK - 1) // SEL_TK, sel_tile, init)
    o_s = acc_s / l_s

    wk = WINDOW + Q_BLOCK
    ws = pl.multiple_of(jnp.maximum(t0 - WINDOW, 0), Q_BLOCK)
    dt = tq - (ws + lax.broadcasted_iota(jnp.int32, (Q_BLOCK, wk), 1))
    bw = per_head(jnp.where((dt >= 0) & (dt <= WINDOW), 0.0, NEG_INF))
    s_w = _dot_nt(q, kw_ref[pl.ds(ws, wk), :].astype(BF16)) + bw
    p_w = jnp.exp(s_w - jnp.max(s_w, -1, keepdims=True))
    o_w = (jnp.dot(p_w.astype(BF16), vw_ref[pl.ds(ws, wk), :].astype(BF16), preferred_element_type=F32)
           / jnp.sum(p_w, -1, keepdims=True))

    gt = gt_ref[...]
    for r in range(GROUP_HEADS):
        sl = slice(r * Q_BLOCK, (r + 1) * Q_BLOCK)
        o = (gt[:, 3 * r:3 * r + 1] * o_c[sl] + gt[:, 3 * r + 1:3 * r + 2] * o_s[sl]
             + gt[:, 3 * r + 2:3 * r + 3] * o_w[sl])
        o_ref[:, r * HEAD_DIM:(r + 1) * HEAD_DIM] = o.astype(BF16)


def _attn_prompt(z, kc, vc, b_, t):
    nq = t // Q_BLOCK
    nc = kc.shape[1]
    assert t % SEL_TK == 0 and t // SEL_BLOCK <= LANE and (t // SEL_BLOCK) % 8 == 0
    e_tab = (jnp.arange(t, dtype=jnp.int32)[:, None] // SEL_BLOCK
             == jnp.arange(LANE, dtype=jnp.int32)[None, :]).astype(BF16)

    def kv_spec(col):
        return pl.BlockSpec((t, HEAD_DIM), lambda b, g, i: (b, col // HEAD_DIM + g))
    cspec = pl.BlockSpec((None, nc, HEAD_DIM), lambda b, g, i: (b * N_GROUPS + g, 0, 0))
    return pl.pallas_call(
        functools.partial(_attn_prompt_kernel, t_len=t, nc=nc),
        grid=(b_, N_GROUPS, nq),
        in_specs=[
            pl.BlockSpec((Q_BLOCK, GROUP_DIM), lambda b, g, i: (b * nq + i, g)),
            kv_spec(COL_KS), kv_spec(COL_VS), kv_spec(COL_KW), kv_spec(COL_VW),
            cspec, cspec,
            pl.BlockSpec((Q_BLOCK, LANE), lambda b, g, i: (b * nq + i, COL_GT // LANE + g)),
            pl.BlockSpec((t, LANE), lambda b, g, i: (0, 0)),
        ],
        out_specs=pl.BlockSpec((Q_BLOCK, GROUP_DIM), lambda b, g, i: (b * nq + i, g)),
        out_shape=jax.ShapeDtypeStruct((b_ * t, N_HEADS * HEAD_DIM), BF16),
        compiler_params=_cparams(3),
        name="attn_prompt",
    )(z, z, z, z, z, kc, vc, z, e_tab)


def _attn_sample_cmp_kernel(q_ref, kc_ref, vc_ref, oc_ref, sel_ref, *, s_len, nc, past, n_sel):
    hr = GROUP_HEADS * s_len
    qf = q_ref[...] * ATTN_SCALE
    tq = past + lax.broadcasted_iota(jnp.int32, (s_len, 1), 0)
    ends = lax.broadcasted_iota(jnp.int32, (s_len, nc), 1) * CMP_STRIDE + (CMP_BLOCK - 1)
    mc1 = jnp.where(ends <= tq, 1.0, 0.0)
    mc = jnp.concatenate([mc1] * GROUP_HEADS, 0)
    m2s = _cmp_to_sel(nc, sel_ref.shape[1])
    for g in range(N_GROUPS):
        q = jnp.concatenate([qf[:, (g * GROUP_HEADS + r) * HEAD_DIM:(g * GROUP_HEADS + r + 1) * HEAD_DIM]
                             for r in range(GROUP_HEADS)], 0).astype(BF16)
        kc = kc_ref[pl.ds(g, nc, stride=N_GROUPS), :].astype(BF16)
        vc = vc_ref[pl.ds(g, nc, stride=N_GROUPS), :].astype(BF16)
        s_c = _dot_nt(q, kc) + _bias(mc)
        p_c = _masked_softmax(s_c, mc)
        oc_ref[g * hr:(g + 1) * hr, :] = jnp.dot(p_c.astype(BF16), vc, preferred_element_type=F32)
        p_sum = p_c[0:s_len]
        for r in range(1, GROUP_HEADS):
            p_sum = p_sum + p_c[r * s_len:(r + 1) * s_len]
        sel = _select_blocks(_split_dot(p_sum, m2s), tq, n_sel)
        sel_ref[g * hr:(g + 1) * hr, :] = jnp.concatenate([sel] * GROUP_HEADS, 0).astype(BF16)


def _attn_sample_cmp(z, row0, kc, vc, db, s_len, past):
    nc = kc.shape[1] // N_GROUPS
    n_sel = -(-(past + s_len) // SEL_BLOCK)
    sel_lanes = -(-n_sel // LANE) * LANE
    rows = N_HEADS * s_len
    rb0 = row0 // s_len
    cspec = pl.BlockSpec((None, nc * N_GROUPS, HEAD_DIM), lambda b: (b, 0, 0))
    return pl.pallas_call(
        functools.partial(_attn_sample_cmp_kernel, s_len=s_len, nc=nc, past=past, n_sel=n_sel),
        grid=(db,),
        in_specs=[pl.BlockSpec((s_len, N_HEADS * HEAD_DIM), lambda b: (rb0 + b, 0)), cspec, cspec],
        out_specs=[pl.BlockSpec((None, rows, HEAD_DIM), lambda b: (b, 0, 0)),
                   pl.BlockSpec((None, rows, sel_lanes), lambda b: (b, 0, 0))],
        out_shape=[jax.ShapeDtypeStruct((db, rows, HEAD_DIM), F32),
                   jax.ShapeDtypeStruct((db, rows, sel_lanes), BF16)],
        compiler_params=_cparams(1),
        name="attn_sample_cmp",
    )(z, kc, vc)


def _attn_sample_kernel(pt_ref, *refs, npg, s_len, past, wb):
    k_pages = refs[:npg]
    v_pages = refs[npg:2 * npg]
    (q_ref, kn_ref, vn_ref, kwn_ref, vwn_ref, swk_ref, swv_ref, oc_ref, sel_ref, gt_ref,
     o_ref, qs_ref, m_ref, l_ref, acc_ref) = refs[2 * npg:]
    c = pl.program_id(1)
    hr = GROUP_HEADS * s_len
    rows = N_GROUPS * hr
    row_id = lax.broadcasted_iota(jnp.int32, (rows, 1), 0)
    row_g = row_id // hr
    row_t = past + row_id % s_len

    @pl.when(c == 0)
    def _():
        qf = q_ref[...] * ATTN_SCALE
        for h in range(N_HEADS):
            qs_ref[h * s_len:(h + 1) * s_len, :] = qf[:, h * HEAD_DIM:(h + 1) * HEAD_DIM]
        m_ref[...] = jnp.full((rows, 1), NEG_INF, F32)
        l_ref[...] = jnp.zeros((rows, 1), F32)
        acc_ref[...] = jnp.zeros((rows, HEAD_DIM), F32)

    q = qs_ref[...].astype(BF16)
    sel = sel_ref[...]
    sel_lanes = sel.shape[1]
    flat = lax.broadcasted_iota(jnp.int32, (rows, PAGE_FLAT), 1)
    gmatch = flat % N_GROUPS == row_g
    first_half = flat < SEL_BLOCK * N_GROUPS
    bpp = PAGE_ROWS // SEL_BLOCK
    pick = jnp.where(lax.broadcasted_iota(jnp.int32, (sel_lanes, LANE), 0)
                     == c * (npg * bpp) + lax.broadcasted_iota(jnp.int32, (sel_lanes, LANE), 1), 1.0, 0.0)
    col_bias = (jnp.dot(sel, pick.astype(BF16), preferred_element_type=F32) - 1.0) * (-NEG_INF)

    scores = []
    for p_ in range(npg):
        bias = jnp.where(gmatch, jnp.where(first_half, col_bias[:, bpp * p_:bpp * p_ + 1],
                                           col_bias[:, bpp * p_ + 1:bpp * p_ + 2]), NEG_INF)
        scores.append(_dot_nt(q, k_pages[p_][...].astype(BF16)) + bias)
    s_max = scores[0]
    for s in scores[1:]:
        s_max = jnp.maximum(s_max, s)
    m_i = m_ref[...]
    m_new = jnp.maximum(m_i, jnp.max(s_max, -1, keepdims=True))
    alpha = jnp.exp(m_i - m_new)
    p_sum = jnp.zeros((rows, PAGE_FLAT), F32)
    acc = alpha * acc_ref[...]
    for p_ in range(npg):
        p = jnp.exp(scores[p_] - m_new)
        p_sum = p_sum + p
        acc = acc + jnp.dot(p.astype(BF16), v_pages[p_][...].astype(BF16), preferred_element_type=F32)
    l_ref[...] = alpha * l_ref[...] + jnp.sum(p_sum, -1, keepdims=True)
    acc_ref[...] = acc
    m_ref[...] = m_new

    @pl.when(c == pl.num_programs(1) - 1)
    def _():
        jn = lax.broadcasted_iota(jnp.int32, (hr, s_len), 1)
        tn = past + lax.broadcasted_iota(jnp.int32, (hr, s_len), 0) % s_len
        bn = jnp.where(past + jn <= tn, 0.0, NEG_INF)
        kn = kn_ref[...]
        vn = vn_ref[...]
        for g in range(N_GROUPS):
            sl = slice(g * hr, (g + 1) * hr)
            cl = slice(g * HEAD_DIM, (g + 1) * HEAD_DIM)
            s = _dot_nt(q[sl], kn[:, cl].astype(BF16)) + bn
            m_i = m_ref[sl, :]
            m_new = jnp.maximum(m_i, jnp.max(s, -1, keepdims=True))
            alpha = jnp.exp(m_i - m_new)
            p = jnp.exp(s - m_new)
            l_ref[sl, :] = alpha * l_ref[sl, :] + jnp.sum(p, -1, keepdims=True)
            acc_ref[sl, :] = alpha * acc_ref[sl, :] + jnp.dot(p.astype(BF16), vn[:, cl].astype(BF16),
                                                               preferred_element_type=F32)
        o_s = acc_ref[...] / l_ref[...]

        wflat = lax.broadcasted_iota(jnp.int32, (rows, wb * N_GROUPS), 1)
        dt = row_t - (past - wb + wflat // N_GROUPS)
        bw = jnp.where((dt >= 0) & (dt <= WINDOW) & (wflat % N_GROUPS == row_g), 0.0, NEG_INF)
        s_w = _dot_nt(q, swk_ref[...].astype(BF16)) + bw
        m_w = jnp.max(s_w, -1, keepdims=True)
        kwn = kwn_ref[...]
        vwn = vwn_ref[...]
        o_w_parts = []
        for g in range(N_GROUPS):
            sl = slice(g * hr, (g + 1) * hr)
            cl = slice(g * HEAD_DIM, (g + 1) * HEAD_DIM)
            s_n = _dot_nt(q[sl], kwn[:, cl].astype(BF16)) + bn
            m_g = jnp.maximum(m_w[sl], jnp.max(s_n, -1, keepdims=True))
            p_o = jnp.exp(s_w[sl] - m_g)
            p_n = jnp.exp(s_n - m_g)
            den = jnp.sum(p_o, -1, keepdims=True) + jnp.sum(p_n, -1, keepdims=True)
            num = (jnp.dot(p_o.astype(BF16), swv_ref[...].astype(BF16), preferred_element_type=F32)
                   + jnp.dot(p_n.astype(BF16), vwn[:, cl].astype(BF16), preferred_element_type=F32))
            o_w_parts.append(num / den)

        gt = gt_ref[...]
        oc = oc_ref[...]
        for g in range(N_GROUPS):
            for r in range(GROUP_HEADS):
                h = g * GROUP_HEADS + r
                sl = slice(h * s_len, (h + 1) * s_len)
                sw = slice(r * s_len, (r + 1) * s_len)
                c0 = g * LANE + 3 * r
                o = (gt[:, c0:c0 + 1] * oc[sl] + gt[:, c0 + 1:c0 + 2] * o_s[sl]
                     + gt[:, c0 + 2:c0 + 3] * o_w_parts[g][sw])
                o_ref[:, h * HEAD_DIM:(h + 1) * HEAD_DIM] = o


def _attn_sample(z, row0, cache_k, cache_v, page_table, layer, swk, swv, oc, sel, s_len, past):
    db, n_pages = page_table.shape
    npg = PAGES_PER_STEP
    wb = swk.shape[2] // N_GROUPS
    rows = N_HEADS * s_len
    rb0 = row0 // s_len
    qd = N_HEADS * HEAD_DIM

    def zspec(width, col):
        return pl.BlockSpec((s_len, width), lambda b, c, pt: (rb0 + b, col // width))

    def bspec(a):
        return pl.BlockSpec((None,) + a.shape[1:], lambda b, c, pt: (b,) + (0,) * (a.ndim - 1))
    in_specs = [_page_spec(layer, p, npg) for p in range(npg)] * 2 + [
        zspec(qd, COL_Q), zspec(KV_DIM, COL_KS), zspec(KV_DIM, COL_VS), zspec(KV_DIM, COL_KW), zspec(KV_DIM, COL_VW),
        pl.BlockSpec((None, None) + swk.shape[2:], lambda b, c, pt: (layer, b, 0, 0)),
        pl.BlockSpec((None, None) + swv.shape[2:], lambda b, c, pt: (layer, b, 0, 0)),
        bspec(oc), bspec(sel), zspec(N_GROUPS * LANE, COL_GT)]
    return pl.pallas_call(
        functools.partial(_attn_sample_kernel, npg=npg, s_len=s_len, past=past, wb=wb),
        grid_spec=pltpu.PrefetchScalarGridSpec(
            num_scalar_prefetch=1,
            grid=(db, n_pages // npg),
            in_specs=in_specs,
            out_specs=pl.BlockSpec((s_len, qd), lambda b, c, pt: (b, 0)),
            scratch_shapes=[pltpu.VMEM((rows, HEAD_DIM), F32), pltpu.VMEM((rows, 1), F32),
                            pltpu.VMEM((rows, 1), F32), pltpu.VMEM((rows, HEAD_DIM), F32)],
        ),
        out_shape=jax.ShapeDtypeStruct((db * s_len, qd), F32),
        compiler_params=_cparams(2),
        name="attn_sample",
    )(page_table, *([cache_k] * npg), *([cache_v] * npg), z, z, z, z, z, swk, swv, oc, sel, z)


def _merge_kernel(o_ref, y_ref, woa_ref, wpb_ref, ga_ref, gb_ref, m_ref):
    a = jnp.dot(o_ref[...], woa_ref[...], preferred_element_type=F32)
    c = jnp.dot(y_ref[...], wpb_ref[...], preferred_element_type=F32)
    m_ref[...] = (ga_ref[...] * a + gb_ref[...] * c).astype(BF16)


def _merge(o, y, w_oa, w_pb, z):
    n = o.shape[0]
    tm = _row_tile(n, (768, 512, 256, 128, 64, 32, 16))
    tn = TN_IN
    return pl.pallas_call(
        _merge_kernel,
        grid=(n // tm, D_MODEL // tn),
        in_specs=[
            pl.BlockSpec((tm, o.shape[1]), lambda i, j: (i, 0)),
            pl.BlockSpec((tm, CONV_DIM), lambda i, j: (i, 0)),
            pl.BlockSpec((w_oa.shape[0], tn), lambda i, j: (0, j)),
            pl.BlockSpec((CONV_DIM, tn), lambda i, j: (0, j)),
            pl.BlockSpec((tm, tn), lambda i, j: (i, COL_GA // tn + j)),
            pl.BlockSpec((tm, tn), lambda i, j: (i, COL_GB // tn + j)),
        ],
        out_specs=pl.BlockSpec((tm, tn), lambda i, j: (i, j)),
        out_shape=jax.ShapeDtypeStruct((n, D_MODEL), BF16),
        compiler_params=_cparams(2),
        name="merge",
    )(o, y, w_oa, w_pb, z, z)


def _post_mixer_kernel(m_ref, x_ref, wout_ref, g_ref, b_ref, rw_ref, rb_ref, x1_ref, te_ref, tw_ref):
    mix = jnp.dot(m_ref[...], wout_ref[...], preferred_element_type=F32)
    x1 = _layer_norm(DEEPNORM_ALPHA * x_ref[...] + mix, g_ref[...], b_ref[...])
    x1_ref[...] = x1
    logits = jnp.dot(x1, rw_ref[...], preferred_element_type=F32, precision=lax.Precision.HIGHEST) + rb_ref[...]
    lane = lax.broadcasted_iota(jnp.int32, logits.shape, 1)
    work = jnp.where(lane < N_EXPERTS, logits, REMOVED)
    te = jnp.zeros(logits.shape, jnp.int32)
    tv = jnp.full(logits.shape, REMOVED, F32)
    for k in range(TOP_K):
        mx = jnp.max(work, -1, keepdims=True)
        first = jnp.min(jnp.where(work == mx, lane, LANE), -1, keepdims=True)
        te = jnp.where(lane == k, first, te)
        tv = jnp.where(lane == k, mx, tv)
        work = jnp.where(lane == first, REMOVED, work)
    e = jnp.where(lane < TOP_K, jnp.exp(tv - jnp.max(tv, -1, keepdims=True)), 0.0)
    te_ref[...] = te
    tw_ref[...] = e / jnp.sum(e, -1, keepdims=True)


def _post_mixer(m, x, w_out, g, b, rw, rb):
    n = m.shape[0]
    tm = _row_tile(n, (768, 512, 256, 128, 64, 32, 16))
    row = pl.BlockSpec((tm, D_MODEL), lambda i: (i, 0))
    vec = pl.BlockSpec((1, D_MODEL), lambda i: (0, 0))
    small = pl.BlockSpec((tm, LANE), lambda i: (i, 0))
    return pl.pallas_call(
        _post_mixer_kernel,
        grid=(n // tm,),
        in_specs=[row, row, pl.BlockSpec((D_MODEL, D_MODEL), lambda i: (0, 0)), vec, vec,
                  pl.BlockSpec((D_MODEL, LANE), lambda i: (0, 0)), pl.BlockSpec((1, LANE), lambda i: (0, 0))],
        out_specs=[row, small, small],
        out_shape=[jax.ShapeDtypeStruct((n, D_MODEL), F32),
                   jax.ShapeDtypeStruct((n, LANE), jnp.int32), jax.ShapeDtypeStruct((n, LANE), F32)],
        compiler_params=_cparams(1),
        name="post_mixer",
    )(m, x, w_out, g, b, rw, rb)


def _deinterleave_kernel(w_ref, perm_ref, g_ref, l_ref):
    perm = perm_ref[...]
    half = DEINT_GROUP // 2
    for c in range(w_ref.shape[1] // DEINT_GROUP):
        wb = w_ref[:, c * DEINT_GROUP:(c + 1) * DEINT_GROUP].astype(BF16)
        d = jnp.dot(wb, perm, preferred_element_type=F32).astype(BF16)
        g_ref[:, c * half:(c + 1) * half] = d[:, :half]
        l_ref[:, c * half:(c + 1) * half] = d[:, half:]


def _deinterleave(w2d):
    rows, cols = w2d.shape
    tm = _row_tile(rows, (DEINT_TM,))
    half = DEINT_GROUP // 2
    r = jnp.arange(DEINT_GROUP, dtype=jnp.int32)[:, None]
    c = jnp.arange(DEINT_GROUP, dtype=jnp.int32)[None, :]
    perm = jnp.where(c < half, r == 2 * c, r == 2 * (c - half) + 1).astype(BF16)
    ospec = pl.BlockSpec((tm, cols // 2), lambda i: (i, 0))
    shp = jax.ShapeDtypeStruct((rows, cols // 2), BF16)
    return pl.pallas_call(
        _deinterleave_kernel,
        grid=(rows // tm,),
        in_specs=[pl.BlockSpec((tm, cols), lambda i: (i, 0)),
                  pl.BlockSpec((DEINT_GROUP, DEINT_GROUP), lambda i: (0, 0))],
        out_specs=[ospec, ospec],
        out_shape=[shp, shp],
        compiler_params=_cparams(1),
        name="deinterleave",
    )(w2d, perm)


def _moe_kernel(be_ref, nu_ref, x_ref, wg_ref, wl_ref, bg_ref, bl_ref, wd_ref, bd_ref, o_ref, xb_ref):
    i = pl.program_id(0)
    t = pl.program_id(1)

    @pl.when((i < nu_ref[0]) & (t == 0))
    def _():
        xb_ref[...] = x_ref[...].astype(BF16)

    @pl.when(i < nu_ref[0])
    def _():
        x = xb_ref[...]
        subs = [slice(h * MOE_SUB, (h + 1) * MOE_SUB) for h in range(MOE_TH // MOE_SUB)]
        ups = [(jnp.dot(x, wg_ref[:, cs], preferred_element_type=F32) + bg_ref[:, cs],
                jnp.dot(x, wl_ref[:, cs], preferred_element_type=F32) + bl_ref[:, cs]) for cs in subs]
        part = None
        for cs, (hg, hl) in zip(subs, ups):
            glu = jnp.minimum(hg, SWIGLU_LIMIT)
            lin = jnp.clip(hl, -SWIGLU_LIMIT, SWIGLU_LIMIT)
            act = glu * _sigmoid(SWIGLU_ALPHA * glu) * (lin + 1.0)
            d = jnp.dot(act.astype(BF16), wd_ref[cs, :].astype(BF16), preferred_element_type=F32)
            part = d if part is None else part + d

        @pl.when(t == 0)
        def _():
            o_ref[...] = part + bd_ref[...]

        @pl.when(t > 0)
        def _():
            o_ref[...] += part


def _moe(x_pad, blk_e, n_used, wg, wl, bg, bl, wd, bd):
    npad = x_pad.shape[0]
    nb = npad // MOE_BM
    nh = D_EXPERT // MOE_TH

    def blk(i, nu):
        return jnp.minimum(i, nu[0] - 1)

    def tt(i, t, nu):
        return jnp.where(i < nu[0], t, nh - 1)
    return pl.pallas_call(
        _moe_kernel,
        grid_spec=pltpu.PrefetchScalarGridSpec(
            num_scalar_prefetch=2,
            grid=(nb, nh),
            in_specs=[
                pl.BlockSpec((MOE_BM, D_MODEL), lambda i, t, be, nu: (blk(i, nu), 0)),
                pl.BlockSpec((None, D_MODEL, MOE_TH), lambda i, t, be, nu: (be[blk(i, nu)], 0, tt(i, t, nu))),
                pl.BlockSpec((None, D_MODEL, MOE_TH), lambda i, t, be, nu: (be[blk(i, nu)], 0, tt(i, t, nu))),
                pl.BlockSpec((None, 1, MOE_TH), lambda i, t, be, nu: (be[blk(i, nu)], 0, tt(i, t, nu))),
                pl.BlockSpec((None, 1, MOE_TH), lambda i, t, be, nu: (be[blk(i, nu)], 0, tt(i, t, nu))),
                pl.BlockSpec((None, MOE_TH, D_MODEL), lambda i, t, be, nu: (be[blk(i, nu)], tt(i, t, nu), 0)),
                pl.BlockSpec((None, 1, D_MODEL), lambda i, t, be, nu: (be[blk(i, nu)], 0, 0)),
            ],
            out_specs=pl.BlockSpec((MOE_BM, D_MODEL), lambda i, t, be, nu: (blk(i, nu), 0)),
            scratch_shapes=[pltpu.VMEM((MOE_BM, D_MODEL), BF16)],
        ),
        out_shape=jax.ShapeDtypeStruct((npad, D_MODEL), F32),
        compiler_params=_cparams(2),
        name="moe",
    )(blk_e, n_used, x_pad, wg, wl, bg, bl, wd, bd)


def _route(top_e, n):
    nk = n * TOP_K
    e_flat = top_e.reshape(-1)
    ck = LANE
    nkp = -(-nk // ck) * ck
    e_pad = jnp.pad(e_flat, (0, nkp - nk), constant_values=-1)
    onehot = (e_pad[:, None] == jnp.arange(N_EXPERTS, dtype=jnp.int32)[None, :]).astype(F32)
    chunks = onehot.reshape(nkp // ck, ck, N_EXPERTS)
    tri = (jnp.arange(ck)[:, None] >= jnp.arange(ck)[None, :]).astype(F32)
    within = jnp.einsum('ij,cjk->cik', tri, chunks)
    tot = jnp.sum(chunks, 1)
    before = jnp.cumsum(tot, 0) - tot
    running = (within + before[:, None, :]).reshape(nkp, N_EXPERTS)[:nk]
    rank = jnp.take_along_axis(running, e_flat[:, None], 1)[:, 0].astype(jnp.int32) - 1
    counts = jnp.sum(tot, 0).astype(jnp.int32)
    padded = (counts + MOE_BM - 1) // MOE_BM * MOE_BM
    pend = jnp.cumsum(padded)
    pstart = pend - padded
    dest = pstart[e_flat] + rank
    npad = -(-(nk + N_EXPERTS * (MOE_BM - 1)) // MOE_BM) * MOE_BM
    nb = npad // MOE_BM
    tok = jnp.repeat(jnp.arange(n, dtype=jnp.int32), TOP_K)
    tok_pad = jnp.zeros((npad,), jnp.int32).at[dest].set(tok)
    blk_e = jnp.minimum(jnp.searchsorted(pend, jnp.arange(nb, dtype=jnp.int32) * MOE_BM, side='right'),
                        N_EXPERTS - 1).astype(jnp.int32)
    n_used = (pend[-1] // MOE_BM).astype(jnp.int32).reshape(1)
    return dest.reshape(n, TOP_K), tok_pad, blk_e, n_used


def _post_moe_kernel(x_ref, y_ref, w_ref, g_ref, b_ref, o_ref):
    w = w_ref[...]
    moe = w[:, 0:1] * y_ref[0]
    for k in range(1, TOP_K):
        moe = moe + w[:, k:k + 1] * y_ref[k]
    o_ref[...] = _layer_norm(DEEPNORM_ALPHA * x_ref[...] + moe, g_ref[...], b_ref[...])


def _post_moe(x1, yg, tw, g, b):
    n = x1.shape[0]
    tm = _row_tile(n, (256, 128, 64, 32, 16))
    row = pl.BlockSpec((tm, D_MODEL), lambda i: (i, 0))
    vec = pl.BlockSpec((1, D_MODEL), lambda i: (0, 0))
    return pl.pallas_call(
        _post_moe_kernel,
        grid=(n // tm,),
        in_specs=[row, pl.BlockSpec((TOP_K, tm, D_MODEL), lambda i: (0, i, 0)),
                  pl.BlockSpec((tm, LANE), lambda i: (i, 0)), vec, vec],
        out_specs=row,
        out_shape=jax.ShapeDtypeStruct((n, D_MODEL), F32),
        compiler_params=_cparams(1),
        name="post_moe",
    )(x1, yg, tw, g, b)


def kernel(x_prompt, x_sample, cache_k_cmp, cache_v_cmp, cache_k_sel, cache_v_sel, state_win_k, state_win_v, state_conv, page_table, w_in, b_in, cmp_pe, cmp_w1, cmp_w2, w_oa, dw_w, dw_b, cn_g, cn_b, w_pb, w_out, ln1_g, ln1_b, router_w, router_b, exp_w_up, exp_b_up, exp_w_down, exp_b_down, ln2_g, ln2_b):
    b_, t, _ = x_prompt.shape
    db, s_len, _ = x_sample.shape
    depth = w_in.shape[0]
    n_pages = page_table.shape[1]
    past = n_pages * cache_k_cmp.shape[2]
    wb = state_win_k.shape[2]
    n_p = b_ * t
    n_s = db * s_len
    n = n_p + n_s
    nw = min(WINDOW, t)

    x = jnp.concatenate([x_prompt.reshape(n_p, D_MODEL), x_sample.reshape(n_s, D_MODEL)], 0)
    pos = jnp.concatenate([jnp.tile(jnp.arange(t, dtype=jnp.int32), b_),
                           jnp.tile(past + jnp.arange(s_len, dtype=jnp.int32), db)])
    cos2, sin2 = _rope_tables(pos)
    nc_p = t // CMP_STRIDE
    nc_s = n_pages * PAGE_CHUNKS
    cmp_cos_p, cmp_sin_p = _rope_tables(jnp.arange(nc_p, dtype=jnp.int32) * CMP_STRIDE + CMP_BLOCK - 1)
    cmp_cos_s, cmp_sin_s = _rope_tables(
        jnp.repeat(jnp.arange(nc_s, dtype=jnp.int32), N_GROUPS) * CMP_STRIDE + CMP_BLOCK - 1)

    def pages(c):
        return c.reshape(c.shape[0], c.shape[1], PAGE_FLAT, HEAD_DIM)
    ck_cmp, cv_cmp, ck_sel, cv_sel = pages(cache_k_cmp), pages(cache_v_cmp), pages(cache_k_sel), pages(cache_v_sel)
    swk_all = state_win_k.reshape(depth, db, wb * N_GROUPS, HEAD_DIM)
    swv_all = state_win_v.reshape(depth, db, wb * N_GROUPS, HEAD_DIM)
    conv_zero = jnp.zeros((b_, CONV_PAD, CONV_DIM), F32)
    conv_state = jnp.pad(state_conv, ((0, 0), (0, 0), (CONV_PAD - (CONV_W - 1), 0), (0, 0)))

    ne = depth * N_EXPERTS
    wg_all, wl_all = _deinterleave(exp_w_up.reshape(ne * D_MODEL, 2 * D_EXPERT))
    wg_all = wg_all.reshape(ne, D_MODEL, D_EXPERT)
    wl_all = wl_all.reshape(ne, D_MODEL, D_EXPERT)
    bu = exp_b_up.reshape(ne, 1, D_EXPERT, 2)
    bg_all, bl_all = bu[..., 0], bu[..., 1]
    wd_all = exp_w_down.reshape(ne, D_EXPERT, D_MODEL)
    bd_all = exp_b_down.reshape(ne, 1, D_MODEL)

    outs = [[] for _ in range(14)]
    for l in range(depth):
        w_in_r, b_in_r = _prep_w_in(w_in[l], b_in[l])
        z = _inproj(x, w_in_r, b_in_r, cos2, sin2)

        w1 = cmp_w1[l].reshape(2, CMP_BLOCK * HEAD_DIM, CMP_HIDDEN).astype(BF16)
        w01 = jnp.concatenate([w1[:, :CHUNK_K], w1[:, CHUNK_K:]], -1)
        pe8 = jnp.broadcast_to(cmp_pe[l].reshape(2, 1, CMP_BLOCK * HEAD_DIM), (2, 8, CMP_BLOCK * HEAD_DIM)).astype(BF16)
        w2 = cmp_w2[l].astype(BF16)
        ab_p = _cmp1_prompt(z, b_, t, w01).reshape(2, b_ * N_GROUPS, nc_p, 2 * CMP_HIDDEN)
        kc_p, vc_p = _cmp2(ab_p[0], ab_p[1], pe8[0], pe8[1], w1[0], w1[1], w2[0], w2[1], cmp_cos_p, cmp_sin_p, 1)
        abk_s, abv_s = _cmp1_sample(ck_cmp, cv_cmp, page_table, l, w01[0], w01[1])
        kc_s, vc_s = _cmp2(abk_s, abv_s, pe8[0], pe8[1], w1[0], w1[1], w2[0], w2[1], cmp_cos_s, cmp_sin_s,
                           N_GROUPS)

        o_p = _attn_prompt(z, kc_p, vc_p, b_, t)
        oc_s, sel_s = _attn_sample_cmp(z, n_p, kc_s, vc_s, db, s_len, past)
        o_s = _attn_sample(z, n_p, ck_sel, cv_sel, page_table, l, swk_all, swv_all, oc_s, sel_s, s_len, past)
        o = jnp.concatenate([o_p, o_s.astype(BF16)], 0)

        y_p, cb_p = _conv(z, 0, b_, t, conv_zero, dw_w[l], dw_b[l], cn_g[l], cn_b[l])
        y_s, cb_s = _conv(z, n_p, db, s_len, conv_state[l], dw_w[l], dw_b[l], cn_g[l], cn_b[l])
        y = jnp.concatenate([y_p, y_s.astype(BF16)], 0)

        m = _merge(o, y, w_oa[l].astype(BF16), w_pb[l].astype(BF16), z)
        rw = jnp.pad(router_w[l], ((0, 0), (0, LANE - N_EXPERTS)))
        rb = jnp.pad(router_b[l], (0, LANE - N_EXPERTS))[None]
        x1, te, tw = _post_mixer(m, x, w_out[l].astype(BF16), ln1_g[l][None], ln1_b[l][None], rw, rb)

        dest, tok_pad, blk_e, n_used = _route(te[:, :TOP_K], n)
        y_pad = _moe(x1[tok_pad], blk_e + l * N_EXPERTS, n_used, wg_all, wl_all, bg_all, bl_all, wd_all, bd_all)
        x = _post_moe(x1, y_pad[dest.T], tw, ln2_g[l][None], ln2_b[l][None])

        def zp(col):
            return z[:n_p, col:col + KV_DIM].reshape(b_, t, N_GROUPS, HEAD_DIM)

        def zs(col):
            return z[n_p:, col:col + KV_DIM].reshape(db, s_len, N_GROUPS, HEAD_DIM)
        vals = [zp(COL_KC), zp(COL_VC), zp(COL_KS), zp(COL_VS), zp(COL_KW)[:, t - nw:], zp(COL_VW)[:, t - nw:],
                cb_p[:, CONV_PAD - (CONV_W - 1):],
                zs(COL_KC), zs(COL_VC), zs(COL_KS), zs(COL_VS),
                jnp.concatenate([state_win_k[l], zs(COL_KW)], 1)[:, s_len:],
                jnp.concatenate([state_win_v[l], zs(COL_VW)], 1)[:, s_len:],
                cb_s[:, CONV_PAD - (CONV_W - 1):]]
        for lst, v in zip(outs, vals):
            lst.append(v)

    return (x[:n_p].reshape(b_, t, D_MODEL), x[n_p:].reshape(db, s_len, D_MODEL),
            *[jnp.stack(v) for v in outs])
```

```python
import functools
import math

import jax
import jax.numpy as jnp
from jax import lax
from jax.experimental import pallas as pl
from jax.experimental.pallas import tpu as pltpu

F32 = jnp.float32
BF16 = jnp.bfloat16

D_MODEL = 2048
N_HEADS = 16
HEAD_DIM = 128
N_GROUPS = 4
GROUP_HEADS = N_HEADS // N_GROUPS
GROUP_DIM = GROUP_HEADS * HEAD_DIM
KV_DIM = N_GROUPS * HEAD_DIM
CMP_BLOCK = 32
CMP_STRIDE = 16
CMP_HIDDEN = 256
SEL_BLOCK = 64
SEL_TOP = 16
WINDOW = 512
Q_BLOCK = 128
CONV_DIM = D_MODEL // 2
CONV_W = 31
CONV_PAD = 32
N_EXPERTS = 32
TOP_K = 4
D_EXPERT = D_MODEL
SWIGLU_LIMIT = 7.0
SWIGLU_ALPHA = 1.702
ROPE_THETA = 10000.0
LN_EPS = 1e-5
NEG_INF = -1e30
REMOVED = -3e38
FORCE_SCORE = 1e9
DEPTH = 2
DEEPNORM_ALPHA = (2 * DEPTH) ** 0.25
ATTN_SCALE = HEAD_DIM ** -0.5
GATE_DIM = 3 * N_HEADS

LANE = 128
TN_IN = 512
COL_Q = 0
COL_KS = 2048
COL_KW = 2560
COL_KC = 3072
COL_VC = 3584
COL_VS = 4096
COL_VW = 4608
COL_U = 5120
COL_GA = 7168
COL_GB = 9216
COL_GT = 11264
P_PAD = 11776
N_ROPE_TILES = COL_KC // TN_IN
N_PLAIN_TILES = (COL_GA - COL_KC) // TN_IN

MOE_BM = 512
MOE_TH = 1024
MOE_SUB = 256
SEL_TK = 512
PAGES_PER_STEP = 8
CMP_PAGES_PER_STEP = 16
DEINT_TM = 1024
DEINT_GROUP = 256
VMEM_LIMIT = 56 * 1024 * 1024


def _cparams(n_axes):
    return pltpu.CompilerParams(dimension_semantics=("arbitrary",) * n_axes,
                                vmem_limit_bytes=VMEM_LIMIT)


def _row_tile(n, cands):
    for c in cands:
        if n % c == 0:
            return c
    raise ValueError(f"no row tile for {n}")


def _sigmoid(x):
    return 1.0 / (1.0 + jnp.exp(-x))


def _layer_norm(x, g, b):
    mu = jnp.mean(x, -1, keepdims=True)
    xc = x - mu
    var = jnp.mean(xc * xc, -1, keepdims=True)
    return xc * lax.rsqrt(var + LN_EPS) * g + b


def _rope_apply(z, cos2, sin2):
    return z * cos2 + pltpu.roll(z, HEAD_DIM // 2, 1) * sin2


def _rope_tables(pos):
    half = HEAD_DIM // 2
    inv = ROPE_THETA ** (-jnp.arange(half, dtype=F32) / half)
    ang = pos.astype(F32)[:, None] * inv[None, :]
    c, s = jnp.cos(ang), jnp.sin(ang)
    return jnp.concatenate([c, c], -1), jnp.concatenate([-s, s], -1)


def _dot_nt(a, b):
    return lax.dot_general(a, b, (((1,), (1,)), ((), ())), preferred_element_type=F32)


def _inproj_kernel(x_ref, w_ref, b_ref, cos_ref, sin_ref, o_ref, xb_ref):
    j = pl.program_id(1)

    @pl.when(j == 0)
    def _():
        xb_ref[...] = x_ref[...].astype(BF16)

    z = jnp.dot(xb_ref[...], w_ref[...], preferred_element_type=F32) + b_ref[...]

    @pl.when(j < N_ROPE_TILES)
    def _():
        c = cos_ref[...]
        s = sin_ref[...]
        for h in range(TN_IN // HEAD_DIM):
            sl = slice(h * HEAD_DIM, (h + 1) * HEAD_DIM)
            o_ref[:, sl] = _rope_apply(z[:, sl], c, s)

    @pl.when((j >= N_ROPE_TILES) & (j < N_ROPE_TILES + N_PLAIN_TILES))
    def _():
        o_ref[...] = z

    @pl.when(j >= N_ROPE_TILES + N_PLAIN_TILES)
    def _():
        o_ref[...] = _sigmoid(z)


def _inproj(x, w, b, cos2, sin2):
    n = x.shape[0]
    tm = _row_tile(n, (768, 512, 256, 128, 64, 32, 16, 8))
    return pl.pallas_call(
        _inproj_kernel,
        grid=(n // tm, P_PAD // TN_IN),
        in_specs=[
            pl.BlockSpec((tm, D_MODEL), lambda i, j: (i, 0)),
            pl.BlockSpec((D_MODEL, TN_IN), lambda i, j: (0, j)),
            pl.BlockSpec((1, TN_IN), lambda i, j: (0, j)),
            pl.BlockSpec((tm, HEAD_DIM), lambda i, j: (i, 0)),
            pl.BlockSpec((tm, HEAD_DIM), lambda i, j: (i, 0)),
        ],
        out_specs=pl.BlockSpec((tm, TN_IN), lambda i, j: (i, j)),
        out_shape=jax.ShapeDtypeStruct((n, P_PAD), F32),
        scratch_shapes=[pltpu.VMEM((tm, D_MODEL), BF16)],
        compiler_params=_cparams(2),
        name="inproj",
    )(x, w, b, cos2, sin2)


def _prep_w_in(w_in, b_in):
    o_q, o_kc, o_vc, o_ks, o_vs, o_kw, o_vw = (0, 2048, 2560, 3072, 3584, 4096, 4608)
    o_gt = 5120
    o_u = o_gt + GATE_DIM
    o_gbr = o_u + 2 * CONV_DIM

    def build(a):
        def sl(o, n):
            return a[..., o:o + n]
        gates = sl(o_gt, GATE_DIM).reshape(a.shape[:-1] + (N_GROUPS, 3 * GROUP_HEADS))
        pad = [(0, 0)] * (gates.ndim - 1) + [(0, LANE - 3 * GROUP_HEADS)]
        gates = jnp.pad(gates, pad).reshape(a.shape[:-1] + (N_GROUPS * LANE,))
        return jnp.concatenate([
            sl(o_q, 2048), sl(o_ks, KV_DIM), sl(o_kw, KV_DIM),
            sl(o_kc, KV_DIM), sl(o_vc, KV_DIM), sl(o_vs, KV_DIM), sl(o_vw, KV_DIM), sl(o_u, 2 * CONV_DIM),
            sl(o_gbr, 2 * D_MODEL), gates], -1)

    return build(w_in).astype(BF16), build(b_in)[None, :]


CONV_RC = 32


def _conv_kernel(*refs, tt, has_prev):
    if has_prev:
        ua_ref, ub_ref, pa_ref, pb_ref, buf_ref, w_ref, b_ref, g_ref, be_ref, y_ref, nb_ref, hx_ref = refs
    else:
        ua_ref, ub_ref, buf_ref, w_ref, b_ref, g_ref, be_ref, y_ref, nb_ref, hx_ref = refs
    i = pl.program_id(1)
    hx_ref[CONV_PAD:CONV_PAD + tt, :] = ua_ref[...] * _sigmoid(ub_ref[...])
    if has_prev:
        @pl.when(i == 0)
        def _():
            hx_ref[0:CONV_PAD, :] = buf_ref[...]

        @pl.when(i > 0)
        def _():
            hx_ref[0:CONV_PAD, :] = pa_ref[...] * _sigmoid(pb_ref[...])
    else:
        hx_ref[0:CONV_PAD, :] = buf_ref[...]
    rc = min(CONV_RC, tt)
    off = CONV_PAD - (CONV_W - 1)
    for c in range(tt // rc):
        acc = jnp.zeros((rc, CONV_DIM), F32) + b_ref[...]
        for r in range(8):
            rows = rc if r == 0 else rc + 8
            slab = None
            for j in range(r, off + CONV_W, 8):
                if j < off:
                    continue
                term = hx_ref[c * rc + j - r:c * rc + j - r + rows, :] * w_ref[j - off:j - off + 1, :]
                slab = term if slab is None else slab + term
            acc = acc + slab[r:r + rc]
        yn = _layer_norm(acc, g_ref[...], be_ref[...])
        y_ref[c * rc:(c + 1) * rc, :] = (yn * _sigmoid(yn)).astype(y_ref.dtype)
    nb_ref[...] = hx_ref[tt:tt + CONV_PAD, :]


def _conv(z, row0, nb_, t, buf, dw_w, dw_b, cn_g, cn_b):
    tt = min(t, 128)
    nt = t // tt
    has_prev = nt > 1
    rb0 = row0 // tt
    ca, cb = COL_U // CONV_DIM, COL_U // CONV_DIM + 1
    in_specs = [
        pl.BlockSpec((tt, CONV_DIM), lambda b, i: (rb0 + b * nt + i, ca)),
        pl.BlockSpec((tt, CONV_DIM), lambda b, i: (rb0 + b * nt + i, cb)),
    ]
    args = [z, z]
    if has_prev:
        pr = tt // CONV_PAD
        pb0 = row0 // CONV_PAD

        def prev_idx(b, i):
            return jnp.maximum(pb0 + (b * nt + i) * pr - 1, 0)
        in_specs += [
            pl.BlockSpec((CONV_PAD, CONV_DIM), lambda b, i: (prev_idx(b, i), ca)),
            pl.BlockSpec((CONV_PAD, CONV_DIM), lambda b, i: (prev_idx(b, i), cb)),
        ]
        args += [z, z]
    in_specs += [
        pl.BlockSpec((None, CONV_PAD, CONV_DIM), lambda b, i: (b, 0, 0)),
        pl.BlockSpec((CONV_PAD, CONV_DIM), lambda b, i: (0, 0)),
        pl.BlockSpec((1, CONV_DIM), lambda b, i: (0, 0)),
        pl.BlockSpec((1, CONV_DIM), lambda b, i: (0, 0)),
        pl.BlockSpec((1, CONV_DIM), lambda b, i: (0, 0)),
    ]
    w_pad = jnp.pad(dw_w, ((0, CONV_PAD - CONV_W), (0, 0)))
    args += [buf, w_pad, dw_b[None], cn_g[None], cn_b[None]]
    return pl.pallas_call(
        functools.partial(_conv_kernel, tt=tt, has_prev=has_prev),
        grid=(nb_, nt),
        in_specs=in_specs,
        out_specs=[
            pl.BlockSpec((tt, CONV_DIM), lambda b, i: (b * nt + i, 0)),
            pl.BlockSpec((None, CONV_PAD, CONV_DIM), lambda b, i: (b, 0, 0)),
        ],
        out_shape=[jax.ShapeDtypeStruct((nb_ * t, CONV_DIM), BF16 if tt % 16 == 0 else F32),
                   jax.ShapeDtypeStruct((nb_, CONV_PAD, CONV_DIM), F32)],
        scratch_shapes=[pltpu.VMEM((tt + CONV_PAD, CONV_DIM), F32)],
        compiler_params=_cparams(2),
        name="conv",
    )(*args)


CHUNK_K = CMP_STRIDE * HEAD_DIM


def _cmp1_prompt_kernel(*refs, nc):
    srcs = refs[:N_GROUPS]
    w_ref, out_ref, c_ref = refs[N_GROUPS:]
    for g in range(N_GROUPS):
        for s in range(CMP_STRIDE):
            c_ref[g * nc:(g + 1) * nc, s * HEAD_DIM:(s + 1) * HEAD_DIM] = (
                srcs[g][pl.ds(s, nc, stride=CMP_STRIDE), :])
    ab = jnp.dot(c_ref[...].astype(BF16), w_ref[...], preferred_element_type=F32)
    for g in range(N_GROUPS):
        out_ref[g] = ab[g * nc:(g + 1) * nc, :]


def _cmp1_prompt(z, b_, t, w01):
    nc = t // CMP_STRIDE
    assert COL_VC == COL_KC + KV_DIM
    return pl.pallas_call(
        functools.partial(_cmp1_prompt_kernel, nc=nc),
        grid=(2, b_),
        in_specs=[pl.BlockSpec((t, HEAD_DIM), functools.partial(
            lambda kv, b, g: (b, COL_KC // HEAD_DIM + kv * N_GROUPS + g), g=g)) for g in range(N_GROUPS)] + [
            pl.BlockSpec((None, CHUNK_K, 2 * CMP_HIDDEN), lambda kv, b: (kv, 0, 0)),
        ],
        out_specs=pl.BlockSpec((None, None, N_GROUPS, nc, 2 * CMP_HIDDEN), lambda kv, b: (kv, b, 0, 0, 0)),
        out_shape=jax.ShapeDtypeStruct((2, b_, N_GROUPS, nc, 2 * CMP_HIDDEN), F32),
        scratch_shapes=[pltpu.VMEM((N_GROUPS * nc, CHUNK_K), F32)],
        compiler_params=_cparams(2),
        name="cmp1_prompt",
    )(*([z] * N_GROUPS), w01)


PAGE_ROWS = 128
PAGE_FLAT = PAGE_ROWS * N_GROUPS
PAGE_CHUNKS = PAGE_ROWS // CMP_STRIDE


def _cmp1_sample_kernel(pt_ref, *refs, npg):
    k_pages = refs[:npg]
    v_pages = refs[npg:2 * npg]
    wk_ref, wv_ref, abk_ref, abv_ref, c_ref = refs[2 * npg:]
    low = lax.broadcasted_iota(jnp.int32, (2 * N_GROUPS, HEAD_DIM), 0) < N_GROUPS
    half = CMP_STRIDE // 2
    for pages, w_ref, out_ref in ((k_pages, wk_ref, abk_ref), (v_pages, wv_ref, abv_ref)):
        for p in range(npg):
            for n2 in range(PAGE_CHUNKS // 2):
                r0 = (p * PAGE_CHUNKS + 2 * n2) * N_GROUPS
                for sp in range(half):
                    ja = (2 * n2 * half + sp) * 8
                    jb = ((2 * n2 + 1) * half + sp) * 8
                    a = pages[p][ja:ja + 8, :]
                    b = pages[p][jb:jb + 8, :]
                    c_ref[r0:r0 + 8, (2 * sp) * HEAD_DIM:(2 * sp + 1) * HEAD_DIM] = (
                        jnp.where(low, a, pltpu.roll(b, N_GROUPS, 0)))
                    c_ref[r0:r0 + 8, (2 * sp + 1) * HEAD_DIM:(2 * sp + 2) * HEAD_DIM] = (
                        jnp.where(low, pltpu.roll(a, N_GROUPS, 0), b))
        out_ref[...] = jnp.dot(c_ref[...].astype(BF16), w_ref[...], preferred_element_type=F32)


def _page_spec(layer, p, npg):
    return pl.BlockSpec((None, None, PAGE_FLAT, HEAD_DIM),
                        lambda b, c, pt: (layer, pt[b, c * npg + p], 0, 0))


def _cmp1_sample(cache_k, cache_v, page_table, layer, wk, wv):
    assert 2 * N_GROUPS == 8 and CMP_STRIDE % 2 == 0
    db, n_pages = page_table.shape
    npg = math.gcd(n_pages, CMP_PAGES_PER_STEP)
    rows = npg * PAGE_CHUNKS * N_GROUPS
    shp = jax.ShapeDtypeStruct((db, n_pages * PAGE_CHUNKS * N_GROUPS, 2 * CMP_HIDDEN), F32)
    ospec = pl.BlockSpec((None, rows, 2 * CMP_HIDDEN), lambda b, c, pt: (b, c, 0))
    wspec = pl.BlockSpec((CHUNK_K, 2 * CMP_HIDDEN), lambda b, c, pt: (0, 0))
    return pl.pallas_call(
        functools.partial(_cmp1_sample_kernel, npg=npg),
        grid_spec=pltpu.PrefetchScalarGridSpec(
            num_scalar_prefetch=1,
            grid=(db, n_pages // npg),
            in_specs=[_page_spec(layer, p, npg) for p in range(npg)] * 2 + [wspec, wspec],
            out_specs=[ospec, ospec],
            scratch_shapes=[pltpu.VMEM((rows, CHUNK_K), F32)],
        ),
        out_shape=[shp, shp],
        compiler_params=_cparams(2),
        name="cmp1_sample",
    )(page_table, *([cache_k] * npg), *([cache_v] * npg), wk, wv)


def _cmp2_kernel(abk_ref, abv_ref, pek_ref, pev_ref, w1k_ref, w1v_ref, w2k_ref, w2v_ref,
                 cos_ref, sin_ref, kc_ref, vc_ref, *, shift):
    def one(ab_ref, pe_ref, w1_ref, w2_ref):
        ab = ab_ref[...]
        pe_term = jnp.dot(pe_ref[...], w1_ref[...], preferred_element_type=F32)[0:1, :]
        h = ab[:, :CMP_HIDDEN] + pltpu.roll(ab[:, CMP_HIDDEN:], ab.shape[0] - shift, 0) + pe_term
        return jnp.dot(jax.nn.gelu(h).astype(BF16), w2_ref[...], preferred_element_type=F32)

    kc_ref[...] = _rope_apply(one(abk_ref, pek_ref, w1k_ref, w2k_ref), cos_ref[...], sin_ref[...])
    vc_ref[...] = one(abv_ref, pev_ref, w1v_ref, w2v_ref)


def _cmp2(abk, abv, pek, pev, w1k, w1v, w2k, w2v, cos2, sin2, shift):
    nb_, rows, _ = abk.shape
    abspec = pl.BlockSpec((None, rows, 2 * CMP_HIDDEN), lambda b: (b, 0, 0))
    ospec = pl.BlockSpec((None, rows, HEAD_DIM), lambda b: (b, 0, 0))
    shp = jax.ShapeDtypeStruct((nb_, rows, HEAD_DIM), F32)

    def full(a):
        return pl.BlockSpec(a.shape, lambda b: (0,) * a.ndim)
    return pl.pallas_call(
        functools.partial(_cmp2_kernel, shift=shift),
        grid=(nb_,),
        in_specs=[abspec, abspec, full(pek), full(pev), full(w1k), full(w1v), full(w2k), full(w2v),
                  full(cos2), full(sin2)],
        out_specs=[ospec, ospec],
        out_shape=[shp, shp],
        compiler_params=_cparams(1),
        name="cmp2",
    )(abk, abv, pek, pev, w1k, w1v, w2k, w2v, cos2, sin2)


def _masked_softmax(s, maskf):
    m = jnp.max(s, -1, keepdims=True)
    p = jnp.exp(s - m) * maskf
    return p / jnp.maximum(jnp.sum(p, -1, keepdims=True), 1e-30)


def _bias(maskf):
    return (maskf - 1.0) * (-NEG_INF)


def _split_dot(p, mat_bf16):
    hi = p.astype(BF16)
    lo = (p - hi.astype(F32)).astype(BF16)
    return (jnp.dot(hi, mat_bf16, preferred_element_type=F32)
            + jnp.dot(lo, mat_bf16, preferred_element_type=F32))


def _cmp_to_sel(n_rows, n_cols):
    cs = lax.broadcasted_iota(jnp.int32, (n_rows, n_cols), 0) * CMP_STRIDE
    ss = lax.broadcasted_iota(jnp.int32, (n_rows, n_cols), 1) * SEL_BLOCK
    return jnp.where((cs < ss + SEL_BLOCK) & (cs + CMP_BLOCK > ss), 1.0, 0.0).astype(BF16)


def _cmp_to_sel_t(n_rows, n_cols):
    ss = lax.broadcasted_iota(jnp.int32, (n_rows, n_cols), 0) * SEL_BLOCK
    cs = lax.broadcasted_iota(jnp.int32, (n_rows, n_cols), 1) * CMP_STRIDE
    return jnp.where((cs < ss + SEL_BLOCK) & (cs + CMP_BLOCK > ss), 1.0, 0.0).astype(BF16)


def _select_blocks(imp, tq, n_sel):
    blk = lax.broadcasted_iota(jnp.int32, imp.shape, 1)
    cur = tq // SEL_BLOCK
    forced = (blk == 0) | (blk == cur) | (blk == cur - 1)
    future = blk * SEL_BLOCK > tq
    work = jnp.where(future, NEG_INF, jnp.where(forced, FORCE_SCORE, imp))
    work = jnp.where(blk < n_sel, work, REMOVED)
    sel = jnp.zeros(imp.shape, F32)
    for _ in range(SEL_TOP):
        mx = jnp.max(work, -1, keepdims=True)
        first = jnp.min(jnp.where(work == mx, blk, imp.shape[1]), -1, keepdims=True)
        pick = blk == first
        sel = jnp.where(pick, 1.0, sel)
        work = jnp.where(pick, REMOVED, work)
    return jnp.where(future, 0.0, sel)


def _select_blocks_t(imp_t, t0, n_top):
    n_sel = imp_t.shape[0]
    blk = lax.broadcasted_iota(jnp.int32, imp_t.shape, 0)
    tq = t0 + lax.broadcasted_iota(jnp.int32, imp_t.shape, 1)
    cur = tq // SEL_BLOCK
    forced = (blk == 0) | (blk == cur) | (blk == cur - 1)
    future = blk * SEL_BLOCK > tq
    work = jnp.where(future, NEG_INF, jnp.where(forced, FORCE_SCORE, imp_t))
    cnt = jnp.zeros(imp_t.shape, F32)
    for k in range(n_sel):
        wk = work[k:k + 1, :]
        ge = jnp.where(wk >= work, 1.0, 0.0)
        gt = jnp.where(wk > work, 1.0, 0.0)
        cnt = cnt + jnp.where(blk > k, ge, gt)
    return jnp.where(future, 0.0, jnp.where(cnt < n_top, 1.0, 0.0))


def _attn_prompt_kernel(q_ref, ks_ref, vs_ref, kw_ref, vw_ref, kc_ref, vc_ref, gt_ref, e_ref, o_ref, *, t_len, nc):
    i = pl.program_id(2)
    t0 = pl.multiple_of(i * Q_BLOCK, Q_BLOCK)
    qf = q_ref[...] * ATTN_SCALE
    q = jnp.concatenate([qf[:, r * HEAD_DIM:(r + 1) * HEAD_DIM] for r in range(GROUP_HEADS)], 0).astype(BF16)
    tq = t0 + lax.broadcasted_iota(jnp.int32, (Q_BLOCK, 1), 0)

    def per_head(a):
        return jnp.concatenate([a] * GROUP_HEADS, 0)

    ends = lax.broadcasted_iota(jnp.int32, (Q_BLOCK, nc), 1) * CMP_STRIDE + (CMP_BLOCK - 1)
    mc = per_head(jnp.where(ends <= tq, 1.0, 0.0))
    s_c = _dot_nt(q, kc_ref[...].astype(BF16)) + _bias(mc)
    p_c = _masked_softmax(s_c, mc)
    o_c = jnp.dot(p_c.astype(BF16), vc_ref[...].astype(BF16), preferred_element_type=F32)
    p_sum = p_c[0:Q_BLOCK]
    for r in range(1, GROUP_HEADS):
        p_sum = p_sum + p_c[r * Q_BLOCK:(r + 1) * Q_BLOCK]
    n_sel = t_len // SEL_BLOCK
    m2s_t = _cmp_to_sel_t(LANE, nc)
    p_hi = p_sum.astype(BF16)
    p_lo = (p_sum - p_hi.astype(F32)).astype(BF16)
    imp_t = _dot_nt(m2s_t, p_hi) + _dot_nt(m2s_t, p_lo)
    sel_t = _select_blocks_t(imp_t[:n_sel], t0, SEL_TOP)
    if n_sel < LANE:
        sel_t = jnp.concatenate([sel_t, jnp.zeros((LANE - n_sel, Q_BLOCK), F32)], 0)
    sel = jnp.transpose(sel_t)

    blk_lane = lax.broadcasted_iota(jnp.int32, (Q_BLOCK, LANE), 1)
    sel_bias = jnp.where(blk_lane < t0 // SEL_BLOCK, (sel - 1.0) * (-NEG_INF), NEG_INF)
    qa = jnp.concatenate([q, per_head(sel_bias.astype(BF16))], 1)
    kj = lax.broadcasted_iota(jnp.int32, (Q_BLOCK, Q_BLOCK), 1)
    qi = lax.broadcasted_iota(jnp.int32, (Q_BLOCK, Q_BLOCK), 0)
    causal = per_head(jnp.where(kj <= qi, 0.0, NEG_INF))
    s_d = _dot_nt(q, ks_ref[pl.ds(t0, Q_BLOCK), :].astype(BF16)) + causal
    m_d = jnp.max(s_d, -1, keepdims=True)
    p_d = jnp.exp(s_d - m_d)
    init = (m_d, jnp.sum(p_d, -1, keepdims=True),
            jnp.dot(p_d.astype(BF16), vs_ref[pl.ds(t0, Q_BLOCK), :].astype(BF16), preferred_element_type=F32))

    def sel_tile(kt, carry):
        m_i, l_i, acc = carry
        k0 = pl.multiple_of(kt * SEL_TK, SEL_TK)
        ka = jnp.concatenate([ks_ref[pl.ds(k0, SEL_TK), :].astype(BF16), e_ref[pl.ds(k0, SEL_TK), :]], 1)
        s = _dot_nt(qa, ka)
        m_new = jnp.maximum(m_i, jnp.max(s, -1, keepdims=True))
        alpha = jnp.exp(m_i - m_new)
        p = jnp.exp(s - m_new)
        l_new = alpha * l_i + jnp.sum(p, -1, keepdims=True)
        acc_new = alpha * acc + jnp.dot(p.astype(BF16), vs_ref[pl.ds(k0, SEL_TK), :].astype(BF16),
                                        preferred_element_type=F32)
        return m_new, l_new, acc_new

    _, l_s, acc_s = lax.fori_loop(0, (t0 + SEL_TK - 1) // SEL_TK, sel_tile, init)
    o_s = acc_s / l_s

    wk = WINDOW + Q_BLOCK
    ws = pl.multiple_of(jnp.maximum(t0 - WINDOW, 0), Q_BLOCK)
    dt = tq - (ws + lax.broadcasted_iota(jnp.int32, (Q_BLOCK, wk), 1))
    bw = per_head(jnp.where((dt >= 0) & (dt <= WINDOW), 0.0, NEG_INF))
    s_w = _dot_nt(q, kw_ref[pl.ds(ws, wk), :].astype(BF16)) + bw
    p_w = jnp.exp(s_w - jnp.max(s_w, -1, keepdims=True))
    o_w = (jnp.dot(p_w.astype(BF16), vw_ref[pl.ds(ws, wk), :].astype(BF16), preferred_element_type=F32)
           / jnp.sum(p_w, -1, keepdims=True))

    gt = gt_ref[...]
    for r in range(GROUP_HEADS):
        sl = slice(r * Q_BLOCK, (r + 1) * Q_BLOCK)
        o = (gt[:, 3 * r:3 * r + 1] * o_c[sl] + gt[:, 3 * r + 1:3 * r + 2] * o_s[sl]
             + gt[:, 3 * r + 2:3 * r + 3] * o_w[sl])
        o_ref[:, r * HEAD_DIM:(r + 1) * HEAD_DIM] = o.astype(BF16)


def _attn_prompt(z, kc, vc, b_, t):
    nq = t // Q_BLOCK
    nc = kc.shape[1]
    assert t % SEL_TK == 0 and t // SEL_BLOCK <= LANE and (t // SEL_BLOCK) % 8 == 0
    e_tab = (jnp.arange(t, dtype=jnp.int32)[:, None] // SEL_BLOCK
             == jnp.arange(LANE, dtype=jnp.int32)[None, :]).astype(BF16)

    def kv_spec(col):
        return pl.BlockSpec((t, HEAD_DIM), lambda b, g, i: (b, col // HEAD_DIM + g))
    cspec = pl.BlockSpec((None, nc, HEAD_DIM), lambda b, g, i: (b * N_GROUPS + g, 0, 0))
    return pl.pallas_call(
        functools.partial(_attn_prompt_kernel, t_len=t, nc=nc),
        grid=(b_, N_GROUPS, nq),
        in_specs=[
            pl.BlockSpec((Q_BLOCK, GROUP_DIM), lambda b, g, i: (b * nq + i, g)),
            kv_spec(COL_KS), kv_spec(COL_VS), kv_spec(COL_KW), kv_spec(COL_VW),
            cspec, cspec,
            pl.BlockSpec((Q_BLOCK, LANE), lambda b, g, i: (b * nq + i, COL_GT // LANE + g)),
            pl.BlockSpec((t, LANE), lambda b, g, i: (0, 0)),
        ],
        out_specs=pl.BlockSpec((Q_BLOCK, GROUP_DIM), lambda b, g, i: (b * nq + i, g)),
        out_shape=jax.ShapeDtypeStruct((b_ * t, N_HEADS * HEAD_DIM), BF16),
        compiler_params=_cparams(3),
        name="attn_prompt",
    )(z, z, z, z, z, kc, vc, z, e_tab)


def _attn_sample_cmp_kernel(q_ref, kc_ref, vc_ref, oc_ref, sel_ref, *, s_len, nc, past, n_sel):
    hr = GROUP_HEADS * s_len
    qf = q_ref[...] * ATTN_SCALE
    tq = past + lax.broadcasted_iota(jnp.int32, (s_len, 1), 0)
    ends = lax.broadcasted_iota(jnp.int32, (s_len, nc), 1) * CMP_STRIDE + (CMP_BLOCK - 1)
    mc1 = jnp.where(ends <= tq, 1.0, 0.0)
    mc = jnp.concatenate([mc1] * GROUP_HEADS, 0)
    m2s = _cmp_to_sel(nc, sel_ref.shape[1])
    for g in range(N_GROUPS):
        q = jnp.concatenate([qf[:, (g * GROUP_HEADS + r) * HEAD_DIM:(g * GROUP_HEADS + r + 1) * HEAD_DIM]
                             for r in range(GROUP_HEADS)], 0).astype(BF16)
        kc = kc_ref[pl.ds(g, nc, stride=N_GROUPS), :].astype(BF16)
        vc = vc_ref[pl.ds(g, nc, stride=N_GROUPS), :].astype(BF16)
        s_c = _dot_nt(q, kc) + _bias(mc)
        p_c = _masked_softmax(s_c, mc)
        oc_ref[g * hr:(g + 1) * hr, :] = jnp.dot(p_c.astype(BF16), vc, preferred_element_type=F32)
        p_sum = p_c[0:s_len]
        for r in range(1, GROUP_HEADS):
            p_sum = p_sum + p_c[r * s_len:(r + 1) * s_len]
        sel = _select_blocks(_split_dot(p_sum, m2s), tq, n_sel)
        sel_ref[g * hr:(g + 1) * hr, :] = jnp.concatenate([sel] * GROUP_HEADS, 0).astype(BF16)


def _attn_sample_cmp(z, row0, kc, vc, db, s_len, past):
    nc = kc.shape[1] // N_GROUPS
    n_sel = -(-(past + s_len) // SEL_BLOCK)
    sel_lanes = -(-n_sel // LANE) * LANE
    rows = N_HEADS * s_len
    rb0 = row0 // s_len
    cspec = pl.BlockSpec((None, nc * N_GROUPS, HEAD_DIM), lambda b: (b, 0, 0))
    return pl.pallas_call(
        functools.partial(_attn_sample_cmp_kernel, s_len=s_len, nc=nc, past=past, n_sel=n_sel),
        grid=(db,),
        in_specs=[pl.BlockSpec((s_len, N_HEADS * HEAD_DIM), lambda b: (rb0 + b, 0)), cspec, cspec],
        out_specs=[pl.BlockSpec((None, rows, HEAD_DIM), lambda b: (b, 0, 0)),
                   pl.BlockSpec((None, rows, sel_lanes), lambda b: (b, 0, 0))],
        out_shape=[jax.ShapeDtypeStruct((db, rows, HEAD_DIM), F32),
                   jax.ShapeDtypeStruct((db, rows, sel_lanes), BF16)],
        compiler_params=_cparams(1),
        name="attn_sample_cmp",
    )(z, kc, vc)


def _attn_sample_kernel(pt_ref, *refs, npg, s_len, past, wb):
    k_pages = refs[:npg]
    v_pages = refs[npg:2 * npg]
    (q_ref, kn_ref, vn_ref, kwn_ref, vwn_ref, swk_ref, swv_ref, oc_ref, sel_ref, gt_ref,
     o_ref, qs_ref, m_ref, l_ref, acc_ref) = refs[2 * npg:]
    c = pl.program_id(1)
    hr = GROUP_HEADS * s_len
    rows = N_GROUPS * hr
    row_id = lax.broadcasted_iota(jnp.int32, (rows, 1), 0)
    row_g = row_id // hr
    row_t = past + row_id % s_len

    @pl.when(c == 0)
    def _():
        qf = q_ref[...] * ATTN_SCALE
        for h in range(N_HEADS):
            qs_ref[h * s_len:(h + 1) * s_len, :] = qf[:, h * HEAD_DIM:(h + 1) * HEAD_DIM]
        m_ref[...] = jnp.full((rows, 1), NEG_INF, F32)
        l_ref[...] = jnp.zeros((rows, 1), F32)
        acc_ref[...] = jnp.zeros((rows, HEAD_DIM), F32)

    q = qs_ref[...].astype(BF16)
    sel = sel_ref[...]
    sel_lanes = sel.shape[1]
    flat = lax.broadcasted_iota(jnp.int32, (rows, PAGE_FLAT), 1)
    gmatch = flat % N_GROUPS == row_g
    first_half = flat < SEL_BLOCK * N_GROUPS
    bpp = PAGE_ROWS // SEL_BLOCK
    pick = jnp.where(lax.broadcasted_iota(jnp.int32, (sel_lanes, LANE), 0)
                     == c * (npg * bpp) + lax.broadcasted_iota(jnp.int32, (sel_lanes, LANE), 1), 1.0, 0.0)
    col_bias = (jnp.dot(sel, pick.astype(BF16), preferred_element_type=F32) - 1.0) * (-NEG_INF)

    scores = []
    for p_ in range(npg):
        bias = jnp.where(gmatch, jnp.where(first_half, col_bias[:, bpp * p_:bpp * p_ + 1],
                                           col_bias[:, bpp * p_ + 1:bpp * p_ + 2]), NEG_INF)
        scores.append(_dot_nt(q, k_pages[p_][...].astype(BF16)) + bias)
    s_max = scores[0]
    for s in scores[1:]:
        s_max = jnp.maximum(s_max, s)
    m_i = m_ref[...]
    m_new = jnp.maximum(m_i, jnp.max(s_max, -1, keepdims=True))
    alpha = jnp.exp(m_i - m_new)
    p_sum = jnp.zeros((rows, PAGE_FLAT), F32)
    acc = alpha * acc_ref[...]
    for p_ in range(npg):
        p = jnp.exp(scores[p_] - m_new)
        p_sum = p_sum + p
        acc = acc + jnp.dot(p.astype(BF16), v_pages[p_][...].astype(BF16), preferred_element_type=F32)
    l_ref[...] = alpha * l_ref[...] + jnp.sum(p_sum, -1, keepdims=True)
    acc_ref[...] = acc
    m_ref[...] = m_new

    @pl.when(c == pl.num_programs(1) - 1)
    def _():
        jn = lax.broadcasted_iota(jnp.int32, (hr, s_len), 1)
        tn = past + lax.broadcasted_iota(jnp.int32, (hr, s_len), 0) % s_len
        bn = jnp.where(past + jn <= tn, 0.0, NEG_INF)
        kn = kn_ref[...]
        vn = vn_ref[...]
        for g in range(N_GROUPS):
            sl = slice(g * hr, (g + 1) * hr)
            cl = slice(g * HEAD_DIM, (g + 1) * HEAD_DIM)
            s = _dot_nt(q[sl], kn[:, cl].astype(BF16)) + bn
            m_i = m_ref[sl, :]
            m_new = jnp.maximum(m_i, jnp.max(s, -1, keepdims=True))
            alpha = jnp.exp(m_i - m_new)
            p = jnp.exp(s - m_new)
            l_ref[sl, :] = alpha * l_ref[sl, :] + jnp.sum(p, -1, keepdims=True)
            acc_ref[sl, :] = alpha * acc_ref[sl, :] + jnp.dot(p.astype(BF16), vn[:, cl].astype(BF16),
                                                               preferred_element_type=F32)
        o_s = acc_ref[...] / l_ref[...]

        wflat = lax.broadcasted_iota(jnp.int32, (rows, wb * N_GROUPS), 1)
        dt = row_t - (past - wb + wflat // N_GROUPS)
        bw = jnp.where((dt >= 0) & (dt <= WINDOW) & (wflat % N_GROUPS == row_g), 0.0, NEG_INF)
        s_w = _dot_nt(q, swk_ref[...].astype(BF16)) + bw
        m_w = jnp.max(s_w, -1, keepdims=True)
        kwn = kwn_ref[...]
        vwn = vwn_ref[...]
        o_w_parts = []
        for g in range(N_GROUPS):
            sl = slice(g * hr, (g + 1) * hr)
            cl = slice(g * HEAD_DIM, (g + 1) * HEAD_DIM)
            s_n = _dot_nt(q[sl], kwn[:, cl].astype(BF16)) + bn
            m_g = jnp.maximum(m_w[sl], jnp.max(s_n, -1, keepdims=True))
            p_o = jnp.exp(s_w[sl] - m_g)
            p_n = jnp.exp(s_n - m_g)
            den = jnp.sum(p_o, -1, keepdims=True) + jnp.sum(p_n, -1, keepdims=True)
            num = (jnp.dot(p_o.astype(BF16), swv_ref[...].astype(BF16), preferred_element_type=F32)
                   + jnp.dot(p_n.astype(BF16), vwn[:, cl].astype(BF16), preferred_element_type=F32))
            o_w_parts.append(num / den)

        gt = gt_ref[...]
        oc = oc_ref[...]
        for g in range(N_GROUPS):
            for r in range(GROUP_HEADS):
                h = g * GROUP_HEADS + r
                sl = slice(h * s_len, (h + 1) * s_len)
                sw = slice(r * s_len, (r + 1) * s_len)
                c0 = g * LANE + 3 * r
                o = (gt[:, c0:c0 + 1] * oc[sl] + gt[:, c0 + 1:c0 + 2] * o_s[sl]
                     + gt[:, c0 + 2:c0 + 3] * o_w_parts[g][sw])
                o_ref[:, h * HEAD_DIM:(h + 1) * HEAD_DIM] = o


def _attn_sample(z, row0, cache_k, cache_v, page_table, layer, swk, swv, oc, sel, s_len, past):
    db, n_pages = page_table.shape
    npg = PAGES_PER_STEP
    wb = swk.shape[2] // N_GROUPS
    rows = N_HEADS * s_len
    rb0 = row0 // s_len
    qd = N_HEADS * HEAD_DIM

    def zspec(width, col):
        return pl.BlockSpec((s_len, width), lambda b, c, pt: (rb0 + b, col // width))

    def bspec(a):
        return pl.BlockSpec((None,) + a.shape[1:], lambda b, c, pt: (b,) + (0,) * (a.ndim - 1))
    in_specs = [_page_spec(layer, p, npg) for p in range(npg)] * 2 + [
        zspec(qd, COL_Q), zspec(KV_DIM, COL_KS), zspec(KV_DIM, COL_VS), zspec(KV_DIM, COL_KW), zspec(KV_DIM, COL_VW),
        pl.BlockSpec((None, None) + swk.shape[2:], lambda b, c, pt: (layer, b, 0, 0)),
        pl.BlockSpec((None, None) + swv.shape[2:], lambda b, c, pt: (layer, b, 0, 0)),
        bspec(oc), bspec(sel), zspec(N_GROUPS * LANE, COL_GT)]
    return pl.pallas_call(
        functools.partial(_attn_sample_kernel, npg=npg, s_len=s_len, past=past, wb=wb),
        grid_spec=pltpu.PrefetchScalarGridSpec(
            num_scalar_prefetch=1,
            grid=(db, n_pages // npg),
            in_specs=in_specs,
            out_specs=pl.BlockSpec((s_len, qd), lambda b, c, pt: (b, 0)),
            scratch_shapes=[pltpu.VMEM((rows, HEAD_DIM), F32), pltpu.VMEM((rows, 1), F32),
                            pltpu.VMEM((rows, 1), F32), pltpu.VMEM((rows, HEAD_DIM), F32)],
        ),
        out_shape=jax.ShapeDtypeStruct((db * s_len, qd), F32),
        compiler_params=_cparams(2),
        name="attn_sample",
    )(page_table, *([cache_k] * npg), *([cache_v] * npg), z, z, z, z, z, swk, swv, oc, sel, z)


def _merge_kernel(o_ref, y_ref, woa_ref, wpb_ref, ga_ref, gb_ref, m_ref):
    a = jnp.dot(o_ref[...], woa_ref[...], preferred_element_type=F32)
    c = jnp.dot(y_ref[...], wpb_ref[...], preferred_element_type=F32)
    m_ref[...] = (ga_ref[...] * a + gb_ref[...] * c).astype(BF16)


def _merge(o, y, w_oa, w_pb, z):
    n = o.shape[0]
    tm = _row_tile(n, (768, 512, 256, 128, 64, 32, 16))
    tn = TN_IN
    return pl.pallas_call(
        _merge_kernel,
        grid=(n // tm, D_MODEL // tn),
        in_specs=[
            pl.BlockSpec((tm, o.shape[1]), lambda i, j: (i, 0)),
            pl.BlockSpec((tm, CONV_DIM), lambda i, j: (i, 0)),
            pl.BlockSpec((w_oa.shape[0], tn), lambda i, j: (0, j)),
            pl.BlockSpec((CONV_DIM, tn), lambda i, j: (0, j)),
            pl.BlockSpec((tm, tn), lambda i, j: (i, COL_GA // tn + j)),
            pl.BlockSpec((tm, tn), lambda i, j: (i, COL_GB // tn + j)),
        ],
        out_specs=pl.BlockSpec((tm, tn), lambda i, j: (i, j)),
        out_shape=jax.ShapeDtypeStruct((n, D_MODEL), BF16),
        compiler_params=_cparams(2),
        name="merge",
    )(o, y, w_oa, w_pb, z, z)


def _post_mixer_kernel(m_ref, x_ref, wout_ref, g_ref, b_ref, rw_ref, rb_ref, x1_ref, te_ref, tw_ref):
    mix = jnp.dot(m_ref[...], wout_ref[...], preferred_element_type=F32)
    x1 = _layer_norm(DEEPNORM_ALPHA * x_ref[...] + mix, g_ref[...], b_ref[...])
    x1_ref[...] = x1
    logits = jnp.dot(x1, rw_ref[...], preferred_element_type=F32, precision=lax.Precision.HIGHEST) + rb_ref[...]
    lane = lax.broadcasted_iota(jnp.int32, logits.shape, 1)
    work = jnp.where(lane < N_EXPERTS, logits, REMOVED)
    te = jnp.zeros(logits.shape, jnp.int32)
    tv = jnp.full(logits.shape, REMOVED, F32)
    for k in range(TOP_K):
        mx = jnp.max(work, -1, keepdims=True)
        first = jnp.min(jnp.where(work == mx, lane, LANE), -1, keepdims=True)
        te = jnp.where(lane == k, first, te)
        tv = jnp.where(lane == k, mx, tv)
        work = jnp.where(lane == first, REMOVED, work)
    e = jnp.where(lane < TOP_K, jnp.exp(tv - jnp.max(tv, -1, keepdims=True)), 0.0)
    te_ref[...] = te
    tw_ref[...] = e / jnp.sum(e, -1, keepdims=True)


def _post_mixer(m, x, w_out, g, b, rw, rb):
    n = m.shape[0]
    tm = _row_tile(n, (768, 512, 256, 128, 64, 32, 16))
    row = pl.BlockSpec((tm, D_MODEL), lambda i: (i, 0))
    vec = pl.BlockSpec((1, D_MODEL), lambda i: (0, 0))
    small = pl.BlockSpec((tm, LANE), lambda i: (i, 0))
    return pl.pallas_call(
        _post_mixer_kernel,
        grid=(n // tm,),
        in_specs=[row, row, pl.BlockSpec((D_MODEL, D_MODEL), lambda i: (0, 0)), vec, vec,
                  pl.BlockSpec((D_MODEL, LANE), lambda i: (0, 0)), pl.BlockSpec((1, LANE), lambda i: (0, 0))],
        out_specs=[row, small, small],
        out_shape=[jax.ShapeDtypeStruct((n, D_MODEL), F32),
                   jax.ShapeDtypeStruct((n, LANE), jnp.int32), jax.ShapeDtypeStruct((n, LANE), F32)],
        compiler_params=_cparams(1),
        name="post_mixer",
    )(m, x, w_out, g, b, rw, rb)


def _deinterleave_kernel(w_ref, perm_ref, g_ref, l_ref):
    perm = perm_ref[...]
    half = DEINT_GROUP // 2
    for c in range(w_ref.shape[1] // DEINT_GROUP):
        wb = w_ref[:, c * DEINT_GROUP:(c + 1) * DEINT_GROUP].astype(BF16)
        d = jnp.dot(wb, perm, preferred_element_type=F32).astype(BF16)
        g_ref[:, c * half:(c + 1) * half] = d[:, :half]
        l_ref[:, c * half:(c + 1) * half] = d[:, half:]


def _deinterleave(w2d):
    rows, cols = w2d.shape
    tm = _row_tile(rows, (DEINT_TM,))
    half = DEINT_GROUP // 2
    r = jnp.arange(DEINT_GROUP, dtype=jnp.int32)[:, None]
    c = jnp.arange(DEINT_GROUP, dtype=jnp.int32)[None, :]
    perm = jnp.where(c < half, r == 2 * c, r == 2 * (c - half) + 1).astype(BF16)
    ospec = pl.BlockSpec((tm, cols // 2), lambda i: (i, 0))
    shp = jax.ShapeDtypeStruct((rows, cols // 2), BF16)
    return pl.pallas_call(
        _deinterleave_kernel,
        grid=(rows // tm,),
        in_specs=[pl.BlockSpec((tm, cols), lambda i: (i, 0)),
                  pl.BlockSpec((DEINT_GROUP, DEINT_GROUP), lambda i: (0, 0))],
        out_specs=[ospec, ospec],
        out_shape=[shp, shp],
        compiler_params=_cparams(1),
        name="deinterleave",
    )(w2d, perm)


def _moe_kernel(be_ref, nu_ref, x_ref, wg_ref, wl_ref, bg_ref, bl_ref, wd_ref, bd_ref, o_ref, xb_ref):
    i = pl.program_id(0)
    t = pl.program_id(1)

    @pl.when((i < nu_ref[0]) & (t == 0))
    def _():
        xb_ref[...] = x_ref[...].astype(BF16)

    @pl.when(i < nu_ref[0])
    def _():
        x = xb_ref[...]
        subs = [slice(h * MOE_SUB, (h + 1) * MOE_SUB) for h in range(MOE_TH // MOE_SUB)]
        ups = [(jnp.dot(x, wg_ref[:, cs], preferred_element_type=F32) + bg_ref[:, cs],
                jnp.dot(x, wl_ref[:, cs], preferred_element_type=F32) + bl_ref[:, cs]) for cs in subs]
        part = None
        for cs, (hg, hl) in zip(subs, ups):
            glu = jnp.minimum(hg, SWIGLU_LIMIT)
            lin = jnp.clip(hl, -SWIGLU_LIMIT, SWIGLU_LIMIT)
            act = glu * _sigmoid(SWIGLU_ALPHA * glu) * (lin + 1.0)
            d = jnp.dot(act.astype(BF16), wd_ref[cs, :].astype(BF16), preferred_element_type=F32)
            part = d if part is None else part + d

        @pl.when(t == 0)
        def _():
            o_ref[...] = part + bd_ref[...]

        @pl.when(t > 0)
        def _():
            o_ref[...] += part


def _moe(x_pad, blk_e, n_used, wg, wl, bg, bl, wd, bd):
    npad = x_pad.shape[0]
    nb = npad // MOE_BM
    nh = D_EXPERT // MOE_TH

    def blk(i, nu):
        return jnp.minimum(i, nu[0] - 1)

    def tt(i, t, nu):
        return jnp.where(i < nu[0], t, nh - 1)
    return pl.pallas_call(
        _moe_kernel,
        grid_spec=pltpu.PrefetchScalarGridSpec(
            num_scalar_prefetch=2,
            grid=(nb, nh),
            in_specs=[
                pl.BlockSpec((MOE_BM, D_MODEL), lambda i, t, be, nu: (blk(i, nu), 0)),
                pl.BlockSpec((None, D_MODEL, MOE_TH), lambda i, t, be, nu: (be[blk(i, nu)], 0, tt(i, t, nu))),
                pl.BlockSpec((None, D_MODEL, MOE_TH), lambda i, t, be, nu: (be[blk(i, nu)], 0, tt(i, t, nu))),
                pl.BlockSpec((None, 1, MOE_TH), lambda i, t, be, nu: (be[blk(i, nu)], 0, tt(i, t, nu))),
                pl.BlockSpec((None, 1, MOE_TH), lambda i, t, be, nu: (be[blk(i, nu)], 0, tt(i, t, nu))),
                pl.BlockSpec((None, MOE_TH, D_MODEL), lambda i, t, be, nu: (be[blk(i, nu)], tt(i, t, nu), 0)),
                pl.BlockSpec((None, 1, D_MODEL), lambda i, t, be, nu: (be[blk(i, nu)], 0, 0)),
            ],
            out_specs=pl.BlockSpec((MOE_BM, D_MODEL), lambda i, t, be, nu: (blk(i, nu), 0)),
            scratch_shapes=[pltpu.VMEM((MOE_BM, D_MODEL), BF16)],
        ),
        out_shape=jax.ShapeDtypeStruct((npad, D_MODEL), F32),
        compiler_params=_cparams(2),
        name="moe",
    )(blk_e, n_used, x_pad, wg, wl, bg, bl, wd, bd)


def _route(top_e, n):
    nk = n * TOP_K
    e_flat = top_e.reshape(-1)
    ck = LANE
    nkp = -(-nk // ck) * ck
    e_pad = jnp.pad(e_flat, (0, nkp - nk), constant_values=-1)
    onehot = (e_pad[:, None] == jnp.arange(N_EXPERTS, dtype=jnp.int32)[None, :]).astype(F32)
    chunks = onehot.reshape(nkp // ck, ck, N_EXPERTS)
    tri = (jnp.arange(ck)[:, None] >= jnp.arange(ck)[None, :]).astype(F32)
    within = jnp.einsum('ij,cjk->cik', tri, chunks)
    tot = jnp.sum(chunks, 1)
    before = jnp.cumsum(tot, 0) - tot
    running = (within + before[:, None, :]).reshape(nkp, N_EXPERTS)[:nk]
    rank = jnp.take_along_axis(running, e_flat[:, None], 1)[:, 0].astype(jnp.int32) - 1
    counts = jnp.sum(tot, 0).astype(jnp.int32)
    padded = (counts + MOE_BM - 1) // MOE_BM * MOE_BM
    pend = jnp.cumsum(padded)
    pstart = pend - padded
    dest = pstart[e_flat] + rank
    npad = -(-(nk + N_EXPERTS * (MOE_BM - 1)) // MOE_BM) * MOE_BM
    nb = npad // MOE_BM
    tok = jnp.repeat(jnp.arange(n, dtype=jnp.int32), TOP_K)
    tok_pad = jnp.zeros((npad,), jnp.int32).at[dest].set(tok)
    blk_e = jnp.minimum(jnp.searchsorted(pend, jnp.arange(nb, dtype=jnp.int32) * MOE_BM, side='right'),
                        N_EXPERTS - 1).astype(jnp.int32)
    n_used = (pend[-1] // MOE_BM).astype(jnp.int32).reshape(1)
    return dest.reshape(n, TOP_K), tok_pad, blk_e, n_used


def _post_moe_kernel(x_ref, y_ref, w_ref, g_ref, b_ref, o_ref):
    w = w_ref[...]
    moe = w[:, 0:1] * y_ref[0]
    for k in range(1, TOP_K):
        moe = moe + w[:, k:k + 1] * y_ref[k]
    o_ref[...] = _layer_norm(DEEPNORM_ALPHA * x_ref[...] + moe, g_ref[...], b_ref[...])


def _post_moe(x1, yg, tw, g, b):
    n = x1.shape[0]
    tm = _row_tile(n, (256, 128, 64, 32, 16))
    row = pl.BlockSpec((tm, D_MODEL), lambda i: (i, 0))
    vec = pl.BlockSpec((1, D_MODEL), lambda i: (0, 0))
    return pl.pallas_call(
        _post_moe_kernel,
        grid=(n // tm,),
        in_specs=[row, pl.BlockSpec((TOP_K, tm, D_MODEL), lambda i: (0, i, 0)),
                  pl.BlockSpec((tm, LANE), lambda i: (i, 0)), vec, vec],
        out_specs=row,
        out_shape=jax.ShapeDtypeStruct((n, D_MODEL), F32),
        compiler_params=_cparams(1),
        name="post_moe",
    )(x1, yg, tw, g, b)


def kernel(x_prompt, x_sample, cache_k_cmp, cache_v_cmp, cache_k_sel, cache_v_sel, state_win_k, state_win_v, state_conv, page_table, w_in, b_in, cmp_pe, cmp_w1, cmp_w2, w_oa, dw_w, dw_b, cn_g, cn_b, w_pb, w_out, ln1_g, ln1_b, router_w, router_b, exp_w_up, exp_b_up, exp_w_down, exp_b_down, ln2_g, ln2_b):
    b_, t, _ = x_prompt.shape
    db, s_len, _ = x_sample.shape
    depth = w_in.shape[0]
    n_pages = page_table.shape[1]
    past = n_pages * cache_k_cmp.shape[2]
    wb = state_win_k.shape[2]
    n_p = b_ * t
    n_s = db * s_len
    n = n_p + n_s
    nw = min(WINDOW, t)

    x = jnp.concatenate([x_prompt.reshape(n_p, D_MODEL), x_sample.reshape(n_s, D_MODEL)], 0)
    pos = jnp.concatenate([jnp.tile(jnp.arange(t, dtype=jnp.int32), b_),
                           jnp.tile(past + jnp.arange(s_len, dtype=jnp.int32), db)])
    cos2, sin2 = _rope_tables(pos)
    nc_p = t // CMP_STRIDE
    nc_s = n_pages * PAGE_CHUNKS
    cmp_cos_p, cmp_sin_p = _rope_tables(jnp.arange(nc_p, dtype=jnp.int32) * CMP_STRIDE + CMP_BLOCK - 1)
    cmp_cos_s, cmp_sin_s = _rope_tables(
        jnp.repeat(jnp.arange(nc_s, dtype=jnp.int32), N_GROUPS) * CMP_STRIDE + CMP_BLOCK - 1)

    def pages(c):
        return c.reshape(c.shape[0], c.shape[1], PAGE_FLAT, HEAD_DIM)
    ck_cmp, cv_cmp, ck_sel, cv_sel = pages(cache_k_cmp), pages(cache_v_cmp), pages(cache_k_sel), pages(cache_v_sel)
    swk_all = state_win_k.reshape(depth, db, wb * N_GROUPS, HEAD_DIM)
    swv_all = state_win_v.reshape(depth, db, wb * N_GROUPS, HEAD_DIM)
    conv_zero = jnp.zeros((b_, CONV_PAD, CONV_DIM), F32)
    conv_state = jnp.pad(state_conv, ((0, 0), (0, 0), (CONV_PAD - (CONV_W - 1), 0), (0, 0)))

    ne = depth * N_EXPERTS
    wg_all, wl_all = _deinterleave(exp_w_up.reshape(ne * D_MODEL, 2 * D_EXPERT))
    wg_all = wg_all.reshape(ne, D_MODEL, D_EXPERT)
    wl_all = wl_all.reshape(ne, D_MODEL, D_EXPERT)
    bu = exp_b_up.reshape(ne, 1, D_EXPERT, 2)
    bg_all, bl_all = bu[..., 0], bu[..., 1]
    wd_all = exp_w_down.reshape(ne, D_EXPERT, D_MODEL)
    bd_all = exp_b_down.reshape(ne, 1, D_MODEL)

    outs = [[] for _ in range(14)]
    for l in range(depth):
        w_in_r, b_in_r = _prep_w_in(w_in[l], b_in[l])
        z = _inproj(x, w_in_r, b_in_r, cos2, sin2)

        w1 = cmp_w1[l].reshape(2, CMP_BLOCK * HEAD_DIM, CMP_HIDDEN).astype(BF16)
        w01 = jnp.concatenate([w1[:, :CHUNK_K], w1[:, CHUNK_K:]], -1)
        pe8 = jnp.broadcast_to(cmp_pe[l].reshape(2, 1, CMP_BLOCK * HEAD_DIM), (2, 8, CMP_BLOCK * HEAD_DIM)).astype(BF16)
        w2 = cmp_w2[l].astype(BF16)
        ab_p = _cmp1_prompt(z, b_, t, w01).reshape(2, b_ * N_GROUPS, nc_p, 2 * CMP_HIDDEN)
        kc_p, vc_p = _cmp2(ab_p[0], ab_p[1], pe8[0], pe8[1], w1[0], w1[1], w2[0], w2[1], cmp_cos_p, cmp_sin_p, 1)
        abk_s, abv_s = _cmp1_sample(ck_cmp, cv_cmp, page_table, l, w01[0], w01[1])
        kc_s, vc_s = _cmp2(abk_s, abv_s, pe8[0], pe8[1], w1[0], w1[1], w2[0], w2[1], cmp_cos_s, cmp_sin_s,
                           N_GROUPS)

        o_p = _attn_prompt(z, kc_p, vc_p, b_, t)
        oc_s, sel_s = _attn_sample_cmp(z, n_p, kc_s, vc_s, db, s_len, past)
        o_s = _attn_sample(z, n_p, ck_sel, cv_sel, page_table, l, swk_all, swv_all, oc_s, sel_s, s_len, past)
        o = jnp.concatenate([o_p, o_s.astype(BF16)], 0)

        y_p, cb_p = _conv(z, 0, b_, t, conv_zero, dw_w[l], dw_b[l], cn_g[l], cn_b[l])
        y_s, cb_s = _conv(z, n_p, db, s_len, conv_state[l], dw_w[l], dw_b[l], cn_g[l], cn_b[l])
        y = jnp.concatenate([y_p, y_s.astype(BF16)], 0)

        m = _merge(o, y, w_oa[l].astype(BF16), w_pb[l].astype(BF16), z)
        rw = jnp.pad(router_w[l], ((0, 0), (0, LANE - N_EXPERTS)))
        rb = jnp.pad(router_b[l], (0, LANE - N_EXPERTS))[None]
        x1, te, tw = _post_mixer(m, x, w_out[l].astype(BF16), ln1_g[l][None], ln1_b[l][None], rw, rb)

        dest, tok_pad, blk_e, n_used = _route(te[:, :TOP_K], n)
        y_pad = _moe(x1[tok_pad], blk_e + l * N_EXPERTS, n_used, wg_all, wl_all, bg_all, bl_all, wd_all, bd_all)
        x = _post_moe(x1, y_pad[dest.T], tw, ln2_g[l][None], ln2_b[l][None])

        def zp(col):
            return z[:n_p, col:col + KV_DIM].reshape(b_, t, N_GROUPS, HEAD_DIM)

        def zs(col):
            return z[n_p:, col:col + KV_DIM].reshape(db, s_len, N_GROUPS, HEAD_DIM)
        vals = [zp(COL_KC), zp(COL_VC), zp(COL_KS), zp(COL_VS), zp(COL_KW)[:, t - nw:], zp(COL_VW)[:, t - nw:],
                cb_p[:, CONV_PAD - (CONV_W - 1):],
                zs(COL_KC), zs(COL_VC), zs(COL_KS), zs(COL_VS),
                jnp.concatenate([state_win_k[l], zs(COL_KW)], 1)[:, s_len:],
                jnp.concatenate([state_win_v[l], zs(COL_VW)], 1)[:, s_len:],
                cb_s[:, CONV_PAD - (CONV_W - 1):]]
        for lst, v in zip(outs, vals):
            lst.append(v)

    return (x[:n_p].reshape(b_, t, D_MODEL), x[n_p:].reshape(db, s_len, D_MODEL),
            *[jnp.stack(v) for v in outs])
```
